```python
import math
import jax
import jax.numpy as jnp
from jax import lax
import numpy as np

D_MODEL = 1024
BATCH = 4
SEQ = 4096
DEPTH = 4

DN_HEADS = 8
DN_DK = 128
DN_DV = 128
DN_CHUNK = 64
CONV_WIDTH = 4
QK_DIM = DN_HEADS * DN_DK
V_DIM = DN_HEADS * DN_DV
N_QKV_DN = 2 * QK_DIM + V_DIM
MOBA_HEADS = 8
MOBA_HD = 64
MOBA_DIM = MOBA_HEADS * MOBA_HD
MOBA_BLOCK = 256
MOBA_TOPK = 3
MOBA_Q_CHUNK = 128
N_BUCKETS = 32
REL_MAX_DIST = 128
N_BRANCH = 2
N_IN = N_QKV_DN + V_DIM + 2 * DN_HEADS + 3 * MOBA_DIM + N_BRANCH * D_MODEL
D_FF = 2816
N_EXPERTS = 8
MOE_TOPK = 2
D_EXPERT = 3584
N_DENSE = (DEPTH + 1) // 2
N_MOE = DEPTH // 2
DEEPNORM_ALPHA = (2 * DEPTH) ** 0.25
DEEPNORM_BETA = (8 * DEPTH) ** -0.25
LN_EPS = 1e-5
RMS_EPS = 1e-6
F32 = jnp.float32

kernel_name = 'hybrid_deltanet_moba_moe_deepnorm'


def _layer_norm(x, g, b):
    xf = x.astype(F32)
    mu = jnp.mean(xf, axis=-1, keepdims=True)
    var = jnp.mean(jnp.square(xf - mu), axis=-1, keepdims=True)
    y = (xf - mu) * lax.rsqrt(var + LN_EPS) * g.astype(F32) + b.astype(F32)
    return y.astype(x.dtype)


def _l2norm(t):
    return t * lax.rsqrt(jnp.sum(jnp.square(t), axis=-1, keepdims=True) + RMS_EPS)


def _causal_conv(x, w):
    width, ch = w.shape
    return lax.conv_general_dilated(
        x, w[:, None, :], window_strides=(1,), padding=[(width - 1, 0)],
        dimension_numbers=('NWC', 'WIO', 'NWC'), feature_group_count=ch)


def _rel_bucket(dist):
    n = jnp.maximum(dist, 0)
    max_exact = N_BUCKETS // 2
    nf = jnp.maximum(n, 1).astype(F32)
    large = max_exact + (jnp.log(nf / max_exact) / math.log(REL_MAX_DIST / max_exact)
                         * (N_BUCKETS - max_exact)).astype(jnp.int32)
    large = jnp.minimum(large, N_BUCKETS - 1)
    return jnp.where(n < max_exact, n, large)


def _gated_delta_rule(q, k, v, g, beta):
    b_, s_, h_, dk = q.shape
    dv = v.shape[-1]
    nc = s_ // DN_CHUNK

    def to_chunks(t):
        t = jnp.moveaxis(t, 2, 1)
        return t.reshape((b_, h_, nc, DN_CHUNK) + t.shape[3:])

    q, k, v, g, beta = (to_chunks(t) for t in (q, k, v, g, beta))
    decay = jnp.cumsum(g, axis=-1)
    pos = jnp.arange(DN_CHUNK)
    lower = pos[:, None] >= pos[None, :]
    strict = pos[:, None] > pos[None, :]
    diff = decay[..., :, None] - decay[..., None, :]
    decay_mask = jnp.exp(jnp.where(lower, diff, -jnp.inf))
    k_beta = k * beta[..., None]
    v_beta = v * beta[..., None]
    kk = jnp.einsum('bhncd,bhnjd->bhncj', k_beta, k)
    a_mat = jnp.eye(DN_CHUNK, dtype=F32) + jnp.where(strict, kk * decay_mask, 0.0)
    rhs = jnp.concatenate([v_beta, k_beta * jnp.exp(decay)[..., None]], axis=-1)
    sol = lax.linalg.triangular_solve(a_mat, rhs, left_side=True, lower=True)
    value, k_cum = sol[..., :dv], sol[..., dv:]
    intra = jnp.einsum('bhncd,bhnjd->bhncj', q, k) * decay_mask
    q_dec = q * jnp.exp(decay)[..., None]
    k_dec = k * jnp.exp(decay[..., -1:] - decay)[..., None]
    g_last = jnp.exp(decay[..., -1])
    xs = tuple(jnp.moveaxis(t, 2, 0) for t in (q_dec, k_dec, value, k_cum, intra, g_last))

    def step(state, inp):
        qd, kd, val, kc, att, gl = inp
        v_new = val - jnp.einsum('bhck,bhkv->bhcv', kc, state)
        o = jnp.einsum('bhck,bhkv->bhcv', qd, state) + jnp.einsum('bhcj,bhjv->bhcv', att, v_new)
        state = state * gl[..., None, None] + jnp.einsum('bhck,bhcv->bhkv', kd, v_new)
        return state, o

    state0 = jnp.zeros((b_, h_, dk, dv), F32)
    _, o = lax.scan(step, state0, xs)
    o = jnp.moveaxis(o, 0, 2).reshape(b_, h_, s_, dv)
    return jnp.moveaxis(o, 1, 2)


def _moba_attention(q, k, v, rel_bias):
    b_, s_, h_, hd = q.shape
    s_pad = -(-s_ // MOBA_BLOCK) * MOBA_BLOCK
    pad = s_pad - s_
    q, k, v = (jnp.pad(jnp.moveaxis(t, 2, 1), ((0, 0), (0, 0), (0, pad), (0, 0))) for t in (q, k, v))
    q = q * (hd ** -0.5)
    nb = s_pad // MOBA_BLOCK
    k_blk = k.reshape(b_, h_, nb, MOBA_BLOCK, hd)
    v_blk = v.reshape(b_, h_, nb, MOBA_BLOCK, hd)
    k_mean = jnp.mean(k_blk.astype(F32), axis=3)
    own = jnp.arange(s_pad) // MOBA_BLOCK
    scores = jnp.einsum('bhsd,bhnd->bhsn', q.astype(F32), k_mean)
    fully_past = jnp.arange(nb)[None, :] < own[:, None]
    scores = jnp.where(fully_past, scores, -jnp.inf)
    n_sel = min(MOBA_TOPK, nb)
    _, sel = lax.top_k(scores, n_sel)
    sel_valid = jnp.arange(n_sel)[None, :] < own[:, None]

    nq = s_pad // MOBA_Q_CHUNK
    q_c = q.reshape(b_, h_, nq, MOBA_Q_CHUNK, hd).transpose(2, 0, 1, 3, 4)
    sel_c = sel.reshape(b_, h_, nq, MOBA_Q_CHUNK, n_sel).transpose(2, 0, 1, 3, 4)
    valid_c = sel_valid.reshape(nq, MOBA_Q_CHUNK, n_sel)
    start_c = jnp.arange(nq, dtype=jnp.int32) * MOBA_Q_CHUNK
    b_ix = jnp.arange(b_)[:, None, None, None]
    h_ix = jnp.arange(h_)[None, :, None, None]
    offs = jnp.arange(MOBA_BLOCK)

    def one_chunk(args):
        qc, sc, vc, start = args
        qpos = start + jnp.arange(MOBA_Q_CHUNK)
        ob = start // MOBA_BLOCK
        k_own = lax.dynamic_index_in_dim(k_blk, ob, axis=2, keepdims=False)
        v_own = lax.dynamic_index_in_dim(v_blk, ob, axis=2, keepdims=False)
        dist_own = qpos[:, None] - (ob * MOBA_BLOCK + offs)[None, :]
        l_own = jnp.einsum('bhqd,bhnd->bhqn', qc, k_own).astype(F32) + rel_bias[:, _rel_bucket(dist_own)]
        l_own = jnp.where(dist_own >= 0, l_own, -jnp.inf)
        k_sel = k_blk[b_ix, h_ix, sc]
        v_sel = v_blk[b_ix, h_ix, sc]
        dist_sel = qpos[:, None, None] - (sc[..., None] * MOBA_BLOCK + offs)
        l_sel = jnp.einsum('bhqd,bhqknd->bhqkn', qc, k_sel).astype(F32) + rel_bias[h_ix[..., None], _rel_bucket(dist_sel)]
        l_sel = jnp.where(vc[:, :, None], l_sel, -jnp.inf)
        logits = jnp.concatenate([l_own, l_sel.reshape(b_, h_, MOBA_Q_CHUNK, n_sel * MOBA_BLOCK)], axis=-1)
        p = jax.nn.softmax(logits, axis=-1).astype(v.dtype)
        p_own = p[..., :MOBA_BLOCK]
        p_sel = p[..., MOBA_BLOCK:].reshape(b_, h_, MOBA_Q_CHUNK, n_sel, MOBA_BLOCK)
        return (jnp.einsum('bhqn,bhnd->bhqd', p_own, v_own)
                + jnp.einsum('bhqkn,bhqknd->bhqd', p_sel, v_sel))

    o = lax.map(one_chunk, (q_c, sel_c, valid_c, start_c))
    o = o.transpose(1, 0, 3, 2, 4).reshape(b_, s_pad, h_, hd)
    return o[:, :s_]


def _swiglu(x, w_gate, w_up, w_down):
    h = jax.nn.silu(x @ w_gate) * (x @ w_up)
    return h @ w_down


def _moe_swiglu(x, router_w, router_b, w_gate, w_up, w_down):
    logits = (x @ router_w).astype(F32) + router_b.astype(F32)
    top_val, top_idx = lax.top_k(logits, MOE_TOPK)
    top_w = jax.nn.softmax(top_val, axis=-1)
    gates = jnp.sum(jax.nn.one_hot(top_idx, N_EXPERTS, dtype=F32) * top_w[..., None], axis=-2)
    gates = gates.astype(x.dtype)
    y = jnp.zeros_like(x)
    for e in range(N_EXPERTS):
        y = y + gates[..., e:e + 1] * _swiglu(x, w_gate[e], w_up[e], w_down[e])
    return y


def setup_inputs(seed: int = 0) -> dict:
    key = jax.random.key(seed)
    ks = jax.random.split(key, 24)

    def nrm(k, shape, scale):
        return jax.random.normal(k, shape, F32) * scale

    x = nrm(ks[0], (BATCH, SEQ, D_MODEL), 1.0)
    w_in = nrm(ks[1], (DEPTH, D_MODEL, N_IN), D_MODEL ** -0.5)
    conv_w = nrm(ks[2], (DEPTH, CONV_WIDTH, N_QKV_DN), CONV_WIDTH ** -0.5)
    a_log = jnp.log(jax.random.uniform(ks[3], (DEPTH, DN_HEADS), F32, 1.0, 16.0))
    dt = jnp.exp(jax.random.uniform(ks[4], (DEPTH, DN_HEADS), F32, math.log(1e-3), math.log(1e-1)))
    dt_bias = dt + jnp.log(-jnp.expm1(-dt))
    dn_norm_w = 1.0 + nrm(ks[5], (DEPTH, DN_DV), 0.02)
    w_up_a = nrm(ks[6], (DEPTH, V_DIM, D_MODEL), V_DIM ** -0.5)
    w_up_b = nrm(ks[7], (DEPTH, MOBA_DIM, D_MODEL), MOBA_DIM ** -0.5)
    w_o = nrm(ks[8], (DEPTH, D_MODEL, D_MODEL), DEEPNORM_BETA * D_MODEL ** -0.5)
    rel_bias = nrm(ks[9], (MOBA_HEADS, N_BUCKETS), 0.3)
    ln1_g = 1.0 + nrm(ks[10], (DEPTH, D_MODEL), 0.02)
    ln1_b = nrm(ks[11], (DEPTH, D_MODEL), 0.02)
    ln2_g = 1.0 + nrm(ks[12], (DEPTH, D_MODEL), 0.02)
    ln2_b = nrm(ks[13], (DEPTH, D_MODEL), 0.02)
    ffn_w_gate = nrm(ks[14], (N_DENSE, D_MODEL, D_FF), D_MODEL ** -0.5)
    ffn_w_up = nrm(ks[15], (N_DENSE, D_MODEL, D_FF), D_MODEL ** -0.5)
    ffn_w_down = nrm(ks[16], (N_DENSE, D_FF, D_MODEL), DEEPNORM_BETA * D_FF ** -0.5)
    router_w = nrm(ks[17], (N_MOE, D_MODEL, N_EXPERTS), D_MODEL ** -0.5)
    router_b = nrm(ks[18], (N_MOE, N_EXPERTS), 0.01)
    exp_w_gate = nrm(ks[19], (N_MOE, N_EXPERTS, D_MODEL, D_EXPERT), D_MODEL ** -0.5)
    exp_w_up = nrm(ks[20], (N_MOE, N_EXPERTS, D_MODEL, D_EXPERT), D_MODEL ** -0.5)
    exp_w_down = nrm(ks[21], (N_MOE, N_EXPERTS, D_EXPERT, D_MODEL), DEEPNORM_BETA * D_EXPERT ** -0.5)
    return {'x': x, 'w_in': w_in, 'conv_w': conv_w, 'a_log': a_log, 'dt_bias': dt_bias,
            'dn_norm_w': dn_norm_w, 'w_up_a': w_up_a, 'w_up_b': w_up_b, 'w_o': w_o,
            'rel_bias': rel_bias, 'ln1_g': ln1_g, 'ln1_b': ln1_b, 'ln2_g': ln2_g, 'ln2_b': ln2_b,
            'ffn_w_gate': ffn_w_gate, 'ffn_w_up': ffn_w_up, 'ffn_w_down': ffn_w_down,
            'router_w': router_w, 'router_b': router_b, 'exp_w_gate': exp_w_gate,
            'exp_w_up': exp_w_up, 'exp_w_down': exp_w_down}


def reference(x, w_in, conv_w, a_log, dt_bias, dn_norm_w, w_up_a, w_up_b, w_o, rel_bias,
              ln1_g, ln1_b, ln2_g, ln2_b, ffn_w_gate, ffn_w_up, ffn_w_down,
              router_w, router_b, exp_w_gate, exp_w_up, exp_w_down):
    b_, s_, _ = x.shape
    c0 = N_QKV_DN
    c1 = c0 + V_DIM
    c2 = c1 + DN_HEADS
    c3 = c2 + DN_HEADS
    c4 = c3 + 3 * MOBA_DIM
    for layer in range(DEPTH):
        proj = jnp.einsum('bsd,dn->bsn', x, w_in[layer])
        qkv_dn, z, a_in, b_in, qkv_mb, gate_in = jnp.split(proj, [c0, c1, c2, c3, c4], axis=-1)

        qkv_dn = jax.nn.silu(_causal_conv(qkv_dn, conv_w[layer].astype(x.dtype))).astype(F32)
        q_a, k_a, v_a = jnp.split(qkv_dn, [QK_DIM, 2 * QK_DIM], axis=-1)
        q_a = _l2norm(q_a.reshape(b_, s_, DN_HEADS, DN_DK)) * (DN_DK ** -0.5)
        k_a = _l2norm(k_a.reshape(b_, s_, DN_HEADS, DN_DK))
        v_a = v_a.reshape(b_, s_, DN_HEADS, DN_DV)
        beta = jax.nn.sigmoid(b_in.astype(F32))
        g = -jnp.exp(a_log[layer].astype(F32)) * jax.nn.softplus(a_in.astype(F32) + dt_bias[layer].astype(F32))
        o_a = _gated_delta_rule(q_a, k_a, v_a, g, beta)
        o_a = (o_a * lax.rsqrt(jnp.mean(jnp.square(o_a), axis=-1, keepdims=True) + RMS_EPS)
               * dn_norm_w[layer].astype(F32)
               * jax.nn.silu(z.astype(F32).reshape(b_, s_, DN_HEADS, DN_DV)))
        y_a = jnp.einsum('bse,ed->bsd', o_a.reshape(b_, s_, V_DIM).astype(x.dtype), w_up_a[layer])

        q_b, k_b, v_b = (t.reshape(b_, s_, MOBA_HEADS, MOBA_HD) for t in jnp.split(qkv_mb, 3, axis=-1))
        o_b = _moba_attention(q_b, k_b, v_b, rel_bias)
        y_b = jnp.einsum('bse,ed->bsd', o_b.reshape(b_, s_, MOBA_DIM), w_up_b[layer])

        g_a, g_b = jnp.split(jax.nn.sigmoid(gate_in), N_BRANCH, axis=-1)
        mix = jnp.einsum('bsd,de->bse', g_a * y_a + g_b * y_b, w_o[layer])
        x = _layer_norm(DEEPNORM_ALPHA * x + mix, ln1_g[layer], ln1_b[layer])

        i = layer // 2
        if layer % 2 == 0:
            ffn = _swiglu(x, ffn_w_gate[i], ffn_w_up[i], ffn_w_down[i])
        else:
            ffn = _moe_swiglu(x, router_w[i], router_b[i], exp_w_gate[i], exp_w_up[i], exp_w_down[i])
        x = _layer_norm(DEEPNORM_ALPHA * x + ffn, ln2_g[layer], ln2_b[layer])
    return x
```

```python
import functools
import math

import numpy as np
import jax
import jax.numpy as jnp
from jax import lax
from jax.experimental import pallas as pl
from jax.experimental.pallas import tpu as pltpu

F32 = jnp.float32
BF16 = jnp.bfloat16
HIGHEST = lax.Precision.HIGHEST

D_MODEL = 1024
DEPTH = 4
DN_HEADS = 8
DN_DK = 128
DN_DV = 128
DN_CHUNK = 64
CONV_WIDTH = 4
QK_DIM = DN_HEADS * DN_DK
V_DIM = DN_HEADS * DN_DV
N_QKV_DN = 2 * QK_DIM + V_DIM
MOBA_HEADS = 8
MOBA_HD = 64
MOBA_DIM = MOBA_HEADS * MOBA_HD
MOBA_BLOCK = 256
MOBA_TOPK = 3
MOBA_Q_CHUNK = 128
N_BUCKETS = 32
REL_MAX_DIST = 128
N_EXPERTS = 8
DEEPNORM_ALPHA = (2 * DEPTH) ** 0.25
LN_EPS = 1e-5
RMS_EPS = 1e-6

LANES = 128
VMEM_LIMIT = 48 * 1024 * 1024

C_QKV = 0
C_Z = N_QKV_DN
C_GATE = C_Z + V_DIM
C_MB = C_GATE + 2 * D_MODEL
N_PROJ = C_MB + 3 * MOBA_DIM


def _bucket_starts():
    n = np.arange(0, 4 * REL_MAX_DIST)
    max_exact = N_BUCKETS // 2
    t = np.log(np.maximum(n, 1) / max_exact) / math.log(REL_MAX_DIST / max_exact) * (N_BUCKETS - max_exact)
    large = np.minimum(max_exact + t.astype(np.int64), N_BUCKETS - 1)
    b = np.where(n < max_exact, n, large)
    assert np.all(np.diff(b) >= 0) and b[-1] == N_BUCKETS - 1
    return [int(np.argmax(b >= k)) for k in range(N_BUCKETS)]


BUCKET_STARTS = _bucket_starts()
assert BUCKET_STARTS[-1] <= REL_MAX_DIST


def _cparams(sem):
    return pltpu.CompilerParams(dimension_semantics=sem, vmem_limit_bytes=VMEM_LIMIT)


def _mm(a, b, prec=None):
    return jnp.dot(a, b, preferred_element_type=F32, precision=prec)


def _mm_nt(a, b, prec=None):
    return lax.dot_general(a, b, (((1,), (1,)), ((), ())), preferred_element_type=F32, precision=prec)


def _mm_tn(a, b, prec=None):
    return lax.dot_general(a, b, (((0,), (0,)), ((), ())), preferred_element_type=F32, precision=prec)


def _layer_norm(y, g, b):
    mu = jnp.mean(y, axis=-1, keepdims=True)
    yc = y - mu
    var = jnp.mean(yc * yc, axis=-1, keepdims=True)
    return yc * lax.rsqrt(var + LN_EPS) * g + b


def _inproj_kernel(x_ref, w_ref, o_ref):
    o_ref[...] = _mm(x_ref[...].astype(BF16), w_ref[...])


def _inproj(x, w, tm=512, tn=512):
    t, d = x.shape
    n = w.shape[1]
    return pl.pallas_call(
        _inproj_kernel,
        grid=(t // tm, n // tn),
        in_specs=[pl.BlockSpec((tm, d), lambda i, j: (i, 0)),
                  pl.BlockSpec((d, tn), lambda i, j: (0, j))],
        out_specs=pl.BlockSpec((tm, tn), lambda i, j: (i, j)),
        out_shape=jax.ShapeDtypeStruct((t, n), F32),
        compiler_params=_cparams(("parallel", "arbitrary")),
        name="inproj",
    )(x, w)


def _gates_kernel(x_ref, w_ref, alog_ref, dtb_ref, o_ref):
    p = _mm(x_ref[...].astype(BF16), w_ref[...])
    z = p + dtb_ref[...]
    softplus = jnp.maximum(z, 0.0) + jnp.log1p(jnp.exp(-jnp.abs(z)))
    g = -jnp.exp(alog_ref[...]) * softplus
    tm = p.shape[0]
    r = lax.broadcasted_iota(jnp.int32, (tm, tm), 0)
    c = lax.broadcasted_iota(jnp.int32, (tm, tm), 1)
    tri = ((r // DN_CHUNK == c // DN_CHUNK) & (r >= c)).astype(F32)
    dc = _mm(tri, g, HIGHEST)
    lane = lax.broadcasted_iota(jnp.int32, p.shape, 1)
    o_ref[...] = jnp.where(lane < DN_HEADS, dc, jax.nn.sigmoid(p))


def _gates(x, w_ab, alog, dtb, tm=256):
    t, d = x.shape
    return pl.pallas_call(
        _gates_kernel,
        grid=(t // tm,),
        in_specs=[pl.BlockSpec((tm, d), lambda i: (i, 0)),
                  pl.BlockSpec((d, LANES), lambda i: (0, 0)),
                  pl.BlockSpec((1, LANES), lambda i: (0, 0)),
                  pl.BlockSpec((1, LANES), lambda i: (0, 0))],
        out_specs=pl.BlockSpec((tm, LANES), lambda i: (i, 0)),
        out_shape=jax.ShapeDtypeStruct((t, LANES), F32),
        compiler_params=_cparams(("parallel",)),
        name="gates",
    )(x, w_ab, alog, dtb)


def _conv_kernel(x_ref, w_ref, o_ref):
    j = pl.program_id(1)
    x = x_ref[...]
    w = w_ref[...]
    row = lax.broadcasted_iota(jnp.int32, x.shape, 0)
    y = x * w[CONV_WIDTH - 1:CONV_WIDTH, :]
    for k in range(1, CONV_WIDTH):
        xs = jnp.where(row >= k, pltpu.roll(x, k, axis=0), 0.0)
        y = y + xs * w[CONV_WIDTH - 1 - k:CONV_WIDTH - k, :]
    y = y * jax.nn.sigmoid(y)
    yn = y * lax.rsqrt(jnp.sum(y * y, axis=-1, keepdims=True) + RMS_EPS)
    scale = jnp.where(j < DN_HEADS, DN_DK ** -0.5, 1.0)
    o_ref[...] = jnp.where(j < 2 * DN_HEADS, yn * scale, y)


def _conv_prep(proj, conv_w, batch, seq):
    nblk = N_QKV_DN // LANES
    return pl.pallas_call(
        _conv_kernel,
        grid=(batch, nblk),
        in_specs=[pl.BlockSpec((seq, LANES), lambda b, j: (b, j)),
                  pl.BlockSpec((CONV_WIDTH, LANES), lambda b, j: (0, j))],
        out_specs=pl.BlockSpec((seq, LANES), lambda b, j: (b, j)),
        out_shape=jax.ShapeDtypeStruct((batch * seq, N_QKV_DN), F32),
        compiler_params=_cparams(("parallel", "parallel")),
        name="conv_prep",
    )(proj, conv_w)


def _dn_kernel(q_ref, k_ref, v_ref, z_ref, gb_ref, dct_ref, nw_ref, o_ref, s_ref, *, cb):
    @pl.when(pl.program_id(1) == 0)
    def _():
        s_ref[...] = jnp.zeros_like(s_ref)

    r = lax.broadcasted_iota(jnp.int32, (DN_CHUNK, DN_CHUNK), 0)
    c = lax.broadcasted_iota(jnp.int32, (DN_CHUNK, DN_CHUNK), 1)
    lower = r >= c
    strict = r > c
    eye = (r == c).astype(F32)
    nw = nw_ref[...]

    def chunk(ci, carry):
        rows = pl.ds(pl.multiple_of(ci * DN_CHUNK, DN_CHUNK), DN_CHUNK)
        gb = gb_ref[rows, :]
        dct = dct_ref[ci]
        for h in range(DN_HEADS):
            cols = slice(h * DN_DK, (h + 1) * DN_DK)
            q = q_ref[rows, cols]
            k = k_ref[rows, cols]
            v = v_ref[rows, cols]
            dc = gb[:, h:h + 1]
            beta = gb[:, DN_HEADS + h:DN_HEADS + h + 1]
            dcr = dct[h:h + 1, :]
            dc_last = dc[DN_CHUNK - 1:DN_CHUNK, :]
            e_col = jnp.exp(dc)
            dm = jnp.exp(jnp.where(lower, dc - dcr, -jnp.inf))
            kb = k * beta
            vb = v * beta
            k16 = k.astype(BF16)
            kk = _mm_nt(kb.astype(BF16), k16)
            nmat = jnp.where(strict, kk * dm, 0.0)
            pw = -nmat
            tinv = eye + pw
            for _ in range(5):
                pw = _mm(pw, pw, HIGHEST)
                tinv = tinv + _mm(tinv, pw, HIGHEST)
            value = _mm(tinv, vb, HIGHEST)
            kcum = _mm(tinv, kb * e_col, HIGHEST)
            att = _mm_nt(q.astype(BF16), k16) * dm
            qd = q * e_col
            kd = k * jnp.exp(dc_last - dc)
            s = s_ref[h]
            s16 = s.astype(BF16)
            v_new = value - _mm(kcum.astype(BF16), s16)
            vn16 = v_new.astype(BF16)
            o = _mm(qd.astype(BF16), s16) + _mm(att.astype(BF16), vn16)
            s_ref[h] = s * jnp.exp(dc_last) + _mm_tn(kd.astype(BF16), vn16)
            zz = z_ref[rows, cols]
            o = o * lax.rsqrt(jnp.mean(o * o, axis=-1, keepdims=True) + RMS_EPS) * nw
            o_ref[rows, cols] = (o * (zz * jax.nn.sigmoid(zz))).astype(o_ref.dtype)
        return carry

    lax.fori_loop(0, cb, chunk, 0)


def _deltanet(qkv, proj, gb, dct, nw, batch, seq, cb=4):
    rows = cb * DN_CHUNK
    nblk = seq // rows
    t = batch * seq
    row_map = lambda col: (lambda b, i: (b * nblk + i, col))
    return pl.pallas_call(
        functools.partial(_dn_kernel, cb=cb),
        grid=(batch, nblk),
        in_specs=[pl.BlockSpec((rows, QK_DIM), row_map(0)),
                  pl.BlockSpec((rows, QK_DIM), row_map(1)),
                  pl.BlockSpec((rows, V_DIM), row_map(2)),
                  pl.BlockSpec((rows, V_DIM), row_map(C_Z // V_DIM)),
                  pl.BlockSpec((rows, LANES), row_map(0)),
                  pl.BlockSpec((cb, DN_HEADS, DN_CHUNK), lambda b, i: (b * nblk + i, 0, 0)),
                  pl.BlockSpec((1, DN_DV), lambda b, i: (0, 0))],
        out_specs=pl.BlockSpec((rows, V_DIM), row_map(0)),
        out_shape=jax.ShapeDtypeStruct((t, V_DIM), BF16),
        scratch_shapes=[pltpu.VMEM((DN_HEADS, DN_DK, DN_DV), F32)],
        compiler_params=_cparams(("parallel", "arbitrary")),
        name="deltanet",
    )(qkv, qkv, qkv, proj, gb, dct, nw)


def _rel_bias_tile(bias_ref, head, dist):
    val = jnp.full(dist.shape, bias_ref[head, 0], F32)
    for b in range(1, N_BUCKETS):
        val = jnp.where(dist >= BUCKET_STARTS[b], bias_ref[head, b], val)
    return val


def _moba_kernel(bias_ref, q_ref, k_ref, v_ref, o_ref, k16_ref, v16_ref, bt_ref, *, seq):
    hp = pl.program_id(1)
    nb = seq // MOBA_BLOCK
    nq = seq // MOBA_Q_CHUNK
    qpb = MOBA_BLOCK // MOBA_Q_CHUNK
    k16_ref[...] = k_ref[...].astype(BF16)
    v16_ref[...] = v_ref[...].astype(BF16)
    kmean = jnp.mean(k_ref[...].reshape(nb, MOBA_BLOCK, LANES), axis=1)
    kmean = jnp.concatenate([kmean, jnp.zeros((LANES - nb, LANES), F32)], axis=0)

    qi = lax.broadcasted_iota(jnp.int32, (MOBA_Q_CHUNK, MOBA_BLOCK), 0)
    kj = lax.broadcasted_iota(jnp.int32, (MOBA_Q_CHUNK, MOBA_BLOCK), 1)
    rel = qi - kj
    for hh in range(2):
        for ti in range(3):
            bt_ref[hh, ti] = _rel_bias_tile(bias_ref, 2 * hp + hh, jnp.maximum(rel + MOBA_Q_CHUNK * ti, 0))
    lane = lax.broadcasted_iota(jnp.int32, (MOBA_Q_CHUNK, LANES), 1)

    def q_chunk(ci, carry):
        rows = pl.ds(pl.multiple_of(ci * MOBA_Q_CHUNK, MOBA_Q_CHUNK), MOBA_Q_CHUNK)
        q2 = q_ref[rows, :] * (MOBA_HD ** -0.5)
        ob = ci // qpb
        half = ci % qpb
        outs = []
        for hh in range(2):
            in_head = (lane >= hh * MOBA_HD) & (lane < (hh + 1) * MOBA_HD)
            qh = jnp.where(in_head, q2, 0.0)
            qh16 = qh.astype(BF16)
            sc = jnp.where(lane < ob, _mm_nt(qh, kmean, HIGHEST), -jnp.inf)
            sel = jnp.zeros(sc.shape, F32)
            for _ in range(MOBA_TOPK):
                m = jnp.max(sc, axis=-1, keepdims=True)
                cand = jnp.where((sc == m) & (sc > -jnp.inf), lane, LANES)
                pick = lane == jnp.min(cand, axis=-1, keepdims=True)
                sel = jnp.where(pick, 1.0, sel)
                sc = jnp.where(pick, -jnp.inf, sc)
            b_far = bias_ref[2 * hp + hh, N_BUCKETS - 1]

            kst = pl.ds(pl.multiple_of(ob * MOBA_BLOCK, MOBA_BLOCK), MOBA_BLOCK)
            causal = (rel + half * MOBA_Q_CHUNK) >= 0
            b_own = jnp.where(half == 0, bt_ref[hh, 0], bt_ref[hh, 1])
            s = jnp.where(causal, _mm_nt(qh16, k16_ref[kst, :]) + b_own, -jnp.inf)
            m0 = jnp.max(s, axis=-1, keepdims=True)
            p = jnp.exp(s - m0)
            l0 = jnp.sum(p, axis=-1, keepdims=True)
            acc0 = _mm(p.astype(BF16), v16_ref[kst, :])

            def past(n, st):
                m_i, l_i, acc = st
                ks = pl.ds(pl.multiple_of(n * MOBA_BLOCK, MOBA_BLOCK), MOBA_BLOCK)
                near = jnp.logical_and(n == ob - 1, half == 0)
                bias = jnp.where(near, bt_ref[hh, 2], b_far)
                chosen = jnp.sum(jnp.where(lane == n, sel, 0.0), axis=-1, keepdims=True) > 0.0
                sn = jnp.where(chosen, _mm_nt(qh16, k16_ref[ks, :]) + bias, -jnp.inf)
                m_new = jnp.maximum(m_i, jnp.max(sn, axis=-1, keepdims=True))
                corr = jnp.exp(m_i - m_new)
                pn = jnp.exp(sn - m_new)
                l_new = l_i * corr + jnp.sum(pn, axis=-1, keepdims=True)
                acc_new = acc * corr + _mm(pn.astype(BF16), v16_ref[ks, :])
                return m_new, l_new, acc_new

            _, l_f, acc_f = lax.fori_loop(0, ob, past, (m0, l0, acc0))
            outs.append(acc_f / l_f)
        o_ref[rows, :] = jnp.where(lane < MOBA_HD, outs[0], outs[1]).astype(o_ref.dtype)
        return carry

    lax.fori_loop(0, nq, q_chunk, 0)


def _moba(proj, rel_bias, batch, seq):
    npair = MOBA_HEADS // 2
    qb, kb, vb = ((C_MB + i * MOBA_DIM) // LANES for i in range(3))
    blk = lambda base: pl.BlockSpec((seq, LANES), lambda b, h: (b, base + h))
    return pl.pallas_call(
        functools.partial(_moba_kernel, seq=seq),
        grid=(batch, npair),
        in_specs=[pl.BlockSpec(memory_space=pltpu.SMEM), blk(qb), blk(kb), blk(vb)],
        out_specs=pl.BlockSpec((seq, LANES), lambda b, h: (b, h)),
        out_shape=jax.ShapeDtypeStruct((batch * seq, MOBA_DIM), BF16),
        scratch_shapes=[pltpu.VMEM((seq, LANES), BF16), pltpu.VMEM((seq, LANES), BF16),
                        pltpu.VMEM((2, 3, MOBA_Q_CHUNK, MOBA_BLOCK), F32)],
        compiler_params=_cparams(("parallel", "parallel")),
        name="moba",
    )(rel_bias, proj, proj, proj)


def _mix_kernel(oa_ref, ob_ref, ga_ref, gb_ref, x_ref, wa_ref, wb_ref, wo_ref, g_ref, b_ref, o_ref):
    ya = _mm(oa_ref[...], wa_ref[...])
    yb = _mm(ob_ref[...], wb_ref[...])
    m = jax.nn.sigmoid(ga_ref[...]) * ya + jax.nn.sigmoid(gb_ref[...]) * yb
    mix = _mm(m.astype(BF16), wo_ref[...])
    o_ref[...] = _layer_norm(DEEPNORM_ALPHA * x_ref[...] + mix, g_ref[...], b_ref[...])


def _mix(oa, ob, proj, x, wa, wb, wo, g, b, tm=512):
    t, d = x.shape
    row = lambda col: (lambda i: (i, col))
    full = lambda a: pl.BlockSpec(a.shape, lambda i: (0, 0))
    return pl.pallas_call(
        _mix_kernel,
        grid=(t // tm,),
        in_specs=[pl.BlockSpec((tm, V_DIM), row(0)), pl.BlockSpec((tm, MOBA_DIM), row(0)),
                  pl.BlockSpec((tm, d), row(C_GATE // d)), pl.BlockSpec((tm, d), row(C_GATE // d + 1)),
                  pl.BlockSpec((tm, d), row(0)), full(wa), full(wb), full(wo), full(g), full(b)],
        out_specs=pl.BlockSpec((tm, d), row(0)),
        out_shape=jax.ShapeDtypeStruct((t, d), F32),
        compiler_params=_cparams(("parallel",)),
        name="mix",
    )(oa, ob, proj, proj, x, wa, wb, wo, g, b)


def _router_kernel(x_ref, w_ref, b_ref, o_ref):
    logits = _mm(x_ref[...], w_ref[...], HIGHEST) + b_ref[...]
    lane = lax.broadcasted_iota(jnp.int32, logits.shape, 1)
    sc = jnp.where(lane < N_EXPERTS, logits, -jnp.inf)
    m1 = jnp.max(sc, axis=-1, keepdims=True)
    p1 = lane == jnp.min(jnp.where(sc == m1, lane, LANES), axis=-1, keepdims=True)
    sc2 = jnp.where(p1, -jnp.inf, sc)
    m2 = jnp.max(sc2, axis=-1, keepdims=True)
    p2 = lane == jnp.min(jnp.where(sc2 == m2, lane, LANES), axis=-1, keepdims=True)
    e2 = jnp.exp(m2 - m1)
    den = 1.0 + e2
    o_ref[...] = jnp.where(p1, 1.0 / den, jnp.where(p2, e2 / den, 0.0))


def _router(x, w, b, tm=512):
    t, d = x.shape
    return pl.pallas_call(
        _router_kernel,
        grid=(t // tm,),
        in_specs=[pl.BlockSpec((tm, d), lambda i: (i, 0)),
                  pl.BlockSpec((d, LANES), lambda i: (0, 0)),
                  pl.BlockSpec((1, LANES), lambda i: (0, 0))],
        out_specs=pl.BlockSpec((tm, LANES), lambda i: (i, 0)),
        out_shape=jax.ShapeDtypeStruct((t, LANES), F32),
        compiler_params=_cparams(("parallel",)),
        name="router",
    )(x, w, b)


def _ffn_kernel(*refs, gated):
    if gated:
        x_ref, gt_ref, wg_ref, wu_ref, wd_ref, g_ref, b_ref, o_ref, acc_ref, x16_ref = refs
    else:
        x_ref, wg_ref, wu_ref, wd_ref, g_ref, b_ref, o_ref, acc_ref, x16_ref = refs
    e = pl.program_id(1)
    j = pl.program_id(2)
    first = jnp.logical_and(e == 0, j == 0)
    last = jnp.logical_and(e == pl.num_programs(1) - 1, j == pl.num_programs(2) - 1)

    @pl.when(first)
    def _():
        acc_ref[...] = jnp.zeros_like(acc_ref)
        x16_ref[...] = x_ref[...].astype(BF16)

    x16 = x16_ref[...]
    hg = _mm(x16, wg_ref[0])
    hu = _mm(x16, wu_ref[0])
    h = hg * jax.nn.sigmoid(hg) * hu
    if gated:
        gates = gt_ref[...]
        lane = lax.broadcasted_iota(jnp.int32, gates.shape, 1)
        h = h * jnp.sum(jnp.where(lane == e, gates, 0.0), axis=-1, keepdims=True)
    acc_ref[...] += _mm(h.astype(BF16), wd_ref[0])

    @pl.when(last)
    def _():
        o_ref[...] = _layer_norm(DEEPNORM_ALPHA * x_ref[...] + acc_ref[...], g_ref[...], b_ref[...])


def _ffn(x, gates, wg, wu, wd, g, b, tm, tf):
    t, d = x.shape
    ne, _, f = wg.shape
    gated = gates is not None
    in_specs = [pl.BlockSpec((tm, d), lambda i, e, j: (i, 0))]
    args = [x]
    if gated:
        in_specs.append(pl.BlockSpec((tm, LANES), lambda i, e, j: (i, 0)))
        args.append(gates)
    in_specs += [pl.BlockSpec((1, d, tf), lambda i, e, j: (e, 0, j)),
                 pl.BlockSpec((1, d, tf), lambda i, e, j: (e, 0, j)),
                 pl.BlockSpec((1, tf, d), lambda i, e, j: (e, j, 0)),
                 pl.BlockSpec((1, d), lambda i, e, j: (0, 0)),
                 pl.BlockSpec((1, d), lambda i, e, j: (0, 0))]
    args += [wg, wu, wd, g, b]
    return pl.pallas_call(
        functools.partial(_ffn_kernel, gated=gated),
        grid=(t // tm, ne, f // tf),
        in_specs=in_specs,
        out_specs=pl.BlockSpec((tm, d), lambda i, e, j: (i, 0)),
        out_shape=jax.ShapeDtypeStruct((t, d), F32),
        scratch_shapes=[pltpu.VMEM((tm, d), F32), pltpu.VMEM((tm, d), BF16)],
        compiler_params=_cparams(("parallel", "arbitrary", "arbitrary")),
        name="moe" if gated else "ffn",
    )(*args)


def _pad_lanes(v):
    v = v.reshape(1, -1).astype(F32)
    return jnp.pad(v, ((0, 0), (0, LANES - v.shape[1])))


def kernel(x, w_in, conv_w, a_log, dt_bias, dn_norm_w, w_up_a, w_up_b, w_o, rel_bias, ln1_g, ln1_b,
           ln2_g, ln2_b, ffn_w_gate, ffn_w_up, ffn_w_down, router_w, router_b, exp_w_gate, exp_w_up,
           exp_w_down):
    batch, seq, d = x.shape
    t = batch * seq
    assert d == D_MODEL and seq % MOBA_BLOCK == 0 and seq // MOBA_BLOCK <= LANES
    c0 = N_QKV_DN
    c1 = c0 + V_DIM
    c2 = c1 + DN_HEADS
    c3 = c2 + DN_HEADS
    c4 = c3 + 3 * MOBA_DIM
    xf = x.reshape(t, d)
    for layer in range(DEPTH):
        w = w_in[layer]
        w_main = jnp.concatenate([w[:, :c1], w[:, c4:], w[:, c3:c4]], axis=1).astype(BF16)
        w_ab = jnp.pad(w[:, c1:c3], ((0, 0), (0, LANES - 2 * DN_HEADS))).astype(BF16)
        proj = _inproj(xf, w_main)
        gb = _gates(xf, w_ab, _pad_lanes(a_log[layer]), _pad_lanes(dt_bias[layer]))
        dct = gb[:, :DN_HEADS].reshape(t // DN_CHUNK, DN_CHUNK, DN_HEADS).transpose(0, 2, 1)
        qkv = _conv_prep(proj, conv_w[layer], batch, seq)
        o_a = _deltanet(qkv, proj, gb, dct, dn_norm_w[layer].reshape(1, DN_DV), batch, seq)
        o_b = _moba(proj, rel_bias, batch, seq)
        xf = _mix(o_a, o_b, proj, xf, w_up_a[layer].astype(BF16), w_up_b[layer].astype(BF16),
                  w_o[layer].astype(BF16), ln1_g[layer].reshape(1, d), ln1_b[layer].reshape(1, d))
        i = layer // 2
        g2 = ln2_g[layer].reshape(1, d)
        b2 = ln2_b[layer].reshape(1, d)
        if layer % 2 == 0:
            xf = _ffn(xf, None, ffn_w_gate[i][None].astype(BF16), ffn_w_up[i][None].astype(BF16),
                      ffn_w_down[i][None].astype(BF16), g2, b2, tm=512, tf=1408)
        else:
            rw = jnp.pad(router_w[i], ((0, 0), (0, LANES - N_EXPERTS)))
            gates = _router(xf, rw, _pad_lanes(router_b[i]))
            xf = _ffn(xf, gates, exp_w_gate[i].astype(BF16), exp_w_up[i].astype(BF16),
                      exp_w_down[i].astype(BF16), g2, b2, tm=1024, tf=512)
    return xf.reshape(batch, seq, d)
```

```python
import functools
import math

import numpy as np
import jax
import jax.numpy as jnp
from jax import lax
from jax.experimental import pallas as pl
from jax.experimental.pallas import tpu as pltpu

F32 = jnp.float32
BF16 = jnp.bfloat16
HIGHEST = lax.Precision.HIGHEST

D_MODEL = 1024
DEPTH = 4
DN_HEADS = 8
DN_DK = 128
DN_DV = 128
DN_CHUNK = 64
CONV_WIDTH = 4
QK_DIM = DN_HEADS * DN_DK
V_DIM = DN_HEADS * DN_DV
N_QKV_DN = 2 * QK_DIM + V_DIM
MOBA_HEADS = 8
MOBA_HD = 64
MOBA_DIM = MOBA_HEADS * MOBA_HD
MOBA_BLOCK = 256
MOBA_TOPK = 3
MOBA_Q_CHUNK = 128
N_BUCKETS = 32
REL_MAX_DIST = 128
N_EXPERTS = 8
DEEPNORM_ALPHA = (2 * DEPTH) ** 0.25
LN_EPS = 1e-5
RMS_EPS = 1e-6

LANES = 128
VMEM_LIMIT = 48 * 1024 * 1024

C_QKV = 0
C_Z = N_QKV_DN
C_GATE = C_Z + V_DIM
C_MB = C_GATE + 2 * D_MODEL
N_PROJ = C_MB + 3 * MOBA_DIM


def _bucket_starts():
    n = np.arange(0, 4 * REL_MAX_DIST)
    max_exact = N_BUCKETS // 2
    t = np.log(np.maximum(n, 1) / max_exact) / math.log(REL_MAX_DIST / max_exact) * (N_BUCKETS - max_exact)
    large = np.minimum(max_exact + t.astype(np.int64), N_BUCKETS - 1)
    b = np.where(n < max_exact, n, large)
    assert np.all(np.diff(b) >= 0) and b[-1] == N_BUCKETS - 1
    return [int(np.argmax(b >= k)) for k in range(N_BUCKETS)]


BUCKET_STARTS = _bucket_starts()
assert BUCKET_STARTS[-1] <= REL_MAX_DIST


def _cparams(sem):
    return pltpu.CompilerParams(dimension_semantics=sem, vmem_limit_bytes=VMEM_LIMIT)


def _mm(a, b, prec=None):
    return jnp.dot(a, b, preferred_element_type=F32, precision=prec)


def _mm_nt(a, b, prec=None):
    return lax.dot_general(a, b, (((1,), (1,)), ((), ())), preferred_element_type=F32, precision=prec)


def _mm_tn(a, b, prec=None):
    return lax.dot_general(a, b, (((0,), (0,)), ((), ())), preferred_element_type=F32, precision=prec)


def _layer_norm(y, g, b):
    mu = jnp.mean(y, axis=-1, keepdims=True)
    yc = y - mu
    var = jnp.mean(yc * yc, axis=-1, keepdims=True)
    return yc * lax.rsqrt(var + LN_EPS) * g + b


def _inproj_kernel(x_ref, w_ref, o_ref):
    o_ref[...] = _mm(x_ref[...].astype(BF16), w_ref[...])


def _inproj(x, w, tm=512, tn=512):
    t, d = x.shape
    n = w.shape[1]
    return pl.pallas_call(
        _inproj_kernel,
        grid=(t // tm, n // tn),
        in_specs=[pl.BlockSpec((tm, d), lambda i, j: (i, 0)),
                  pl.BlockSpec((d, tn), lambda i, j: (0, j))],
        out_specs=pl.BlockSpec((tm, tn), lambda i, j: (i, j)),
        out_shape=jax.ShapeDtypeStruct((t, n), F32),
        compiler_params=_cparams(("parallel", "arbitrary")),
        name="inproj",
    )(x, w)


def _gates_kernel(x_ref, w_ref, alog_ref, dtb_ref, o_ref):
    p = _mm(x_ref[...].astype(BF16), w_ref[...])
    z = p + dtb_ref[...]
    softplus = jnp.maximum(z, 0.0) + jnp.log1p(jnp.exp(-jnp.abs(z)))
    g = -jnp.exp(alog_ref[...]) * softplus
    tm = p.shape[0]
    r = lax.broadcasted_iota(jnp.int32, (tm, tm), 0)
    c = lax.broadcasted_iota(jnp.int32, (tm, tm), 1)
    tri = ((r // DN_CHUNK == c // DN_CHUNK) & (r >= c)).astype(F32)
    dc = _mm(tri, g, HIGHEST)
    lane = lax.broadcasted_iota(jnp.int32, p.shape, 1)
    o_ref[...] = jnp.where(lane < DN_HEADS, dc, jax.nn.sigmoid(p))


def _gates(x, w_ab, alog, dtb, tm=256):
    t, d = x.shape
    return pl.pallas_call(
        _gates_kernel,
        grid=(t // tm,),
        in_specs=[pl.BlockSpec((tm, d), lambda i: (i, 0)),
                  pl.BlockSpec((d, LANES), lambda i: (0, 0)),
                  pl.BlockSpec((1, LANES), lambda i: (0, 0)),
                  pl.BlockSpec((1, LANES), lambda i: (0, 0))],
        out_specs=pl.BlockSpec((tm, LANES), lambda i: (i, 0)),
        out_shape=jax.ShapeDtypeStruct((t, LANES), F32),
        compiler_params=_cparams(("parallel",)),
        name="gates",
    )(x, w_ab, alog, dtb)


def _conv_kernel(x_ref, w_ref, o_ref):
    j = pl.program_id(1)
    x = x_ref[...]
    w = w_ref[...]
    row = lax.broadcasted_iota(jnp.int32, x.shape, 0)
    y = x * w[CONV_WIDTH - 1:CONV_WIDTH, :]
    for k in range(1, CONV_WIDTH):
        xs = jnp.where(row >= k, pltpu.roll(x, k, axis=0), 0.0)
        y = y + xs * w[CONV_WIDTH - 1 - k:CONV_WIDTH - k, :]
    y = y * jax.nn.sigmoid(y)
    yn = y * lax.rsqrt(jnp.sum(y * y, axis=-1, keepdims=True) + RMS_EPS)
    scale = jnp.where(j < DN_HEADS, DN_DK ** -0.5, 1.0)
    o_ref[...] = jnp.where(j < 2 * DN_HEADS, yn * scale, y)


def _conv_prep(proj, conv_w, batch, seq):
    nblk = N_QKV_DN // LANES
    return pl.pallas_call(
        _conv_kernel,
        grid=(batch, nblk),
        in_specs=[pl.BlockSpec((seq, LANES), lambda b, j: (b, j)),
                  pl.BlockSpec((CONV_WIDTH, LANES), lambda b, j: (0, j))],
        out_specs=pl.BlockSpec((seq, LANES), lambda b, j: (b, j)),
        out_shape=jax.ShapeDtypeStruct((batch * seq, N_QKV_DN), F32),
        compiler_params=_cparams(("parallel", "parallel")),
        name="conv_prep",
    )(proj, conv_w)


def _dn_kernel(q_ref, k_ref, v_ref, z_ref, gb_ref, dct_ref, nw_ref, o_ref, s_ref, *, cb):
    @pl.when(pl.program_id(1) == 0)
    def _():
        s_ref[...] = jnp.zeros_like(s_ref)

    r = lax.broadcasted_iota(jnp.int32, (DN_CHUNK, DN_CHUNK), 0)
    c = lax.broadcasted_iota(jnp.int32, (DN_CHUNK, DN_CHUNK), 1)
    lower = r >= c
    strict = r > c
    eye = (r == c).astype(F32)
    nw = nw_ref[...]

    def chunk(ci, carry):
        rows = pl.ds(pl.multiple_of(ci * DN_CHUNK, DN_CHUNK), DN_CHUNK)
        gb = gb_ref[rows, :]
        dct = dct_ref[ci]
        for h in range(DN_HEADS):
            cols = slice(h * DN_DK, (h + 1) * DN_DK)
            q = q_ref[rows, cols]
            k = k_ref[rows, cols]
            v = v_ref[rows, cols]
            dc = gb[:, h:h + 1]
            beta = gb[:, DN_HEADS + h:DN_HEADS + h + 1]
            dcr = dct[h:h + 1, :]
            dc_last = dc[DN_CHUNK - 1:DN_CHUNK, :]
            e_col = jnp.exp(dc)
            dm = jnp.exp(jnp.where(lower, dc - dcr, -jnp.inf))
            kb = k * beta
            vb = v * beta
            k16 = k.astype(BF16)
            kk = _mm_nt(kb.astype(BF16), k16)
            nmat = jnp.where(strict, kk * dm, 0.0)
            pw = -nmat
            tinv = eye + pw
            for _ in range(5):
                pw = _mm(pw, pw, HIGHEST)
                tinv = tinv + _mm(tinv, pw, HIGHEST)
            value = _mm(tinv, vb, HIGHEST)
            kcum = _mm(tinv, kb * e_col, HIGHEST)
            att = _mm_nt(q.astype(BF16), k16) * dm
            qd = q * e_col
            kd = k * jnp.exp(dc_last - dc)
            s = s_ref[h]
            s16 = s.astype(BF16)
            v_new = value - _mm(kcum.astype(BF16), s16)
            vn16 = v_new.astype(BF16)
            o = _mm(qd.astype(BF16), s16) + _mm(att.astype(BF16), vn16)
            s_ref[h] = s * jnp.exp(dc_last) + _mm_tn(kd.astype(BF16), vn16)
            zz = z_ref[rows, cols]
            o = o * lax.rsqrt(jnp.mean(o * o, axis=-1, keepdims=True) + RMS_EPS) * nw
            o_ref[rows, cols] = (o * (zz * jax.nn.sigmoid(zz))).astype(o_ref.dtype)
        return carry

    lax.fori_loop(0, cb, chunk, 0)


def _deltanet(qkv, proj, gb, dct, nw, batch, seq, cb=4):
    rows = cb * DN_CHUNK
    nblk = seq // rows
    t = batch * seq
    row_map = lambda col: (lambda b, i: (b * nblk + i, col))
    return pl.pallas_call(
        functools.partial(_dn_kernel, cb=cb),
        grid=(batch, nblk),
        in_specs=[pl.BlockSpec((rows, QK_DIM), row_map(0)),
                  pl.BlockSpec((rows, QK_DIM), row_map(1)),
                  pl.BlockSpec((rows, V_DIM), row_map(2)),
                  pl.BlockSpec((rows, V_DIM), row_map(C_Z // V_DIM)),
                  pl.BlockSpec((rows, LANES), row_map(0)),
                  pl.BlockSpec((cb, DN_HEADS, DN_CHUNK), lambda b, i: (b * nblk + i, 0, 0)),
                  pl.BlockSpec((1, DN_DV), lambda b, i: (0, 0))],
        out_specs=pl.BlockSpec((rows, V_DIM), row_map(0)),
        out_shape=jax.ShapeDtypeStruct((t, V_DIM), BF16),
        scratch_shapes=[pltpu.VMEM((DN_HEADS, DN_DK, DN_DV), F32)],
        compiler_params=_cparams(("parallel", "arbitrary")),
        name="deltanet",
    )(qkv, qkv, qkv, proj, gb, dct, nw)


def _rel_bias_tile(bias_ref, head, dist):
    val = jnp.full(dist.shape, bias_ref[head, 0], F32)
    for b in range(1, N_BUCKETS):
        val = jnp.where(dist >= BUCKET_STARTS[b], bias_ref[head, b], val)
    return val


MOBA_NEG = -1e30
MOBA_PAIR = 2 * MOBA_BLOCK


def _moba_kernel(bias_ref, q_ref, k_ref, v_ref, o_ref, ka_ref, v16_ref, bt_ref, s_ref, *, seq):
    hp = pl.program_id(1)
    nb = seq // MOBA_BLOCK
    k = k_ref[...]
    v16_ref[...] = v_ref[...].astype(BF16)
    kmean = jnp.mean(k.reshape(nb, MOBA_BLOCK, LANES), axis=1)
    lane_k = lax.broadcasted_iota(jnp.int32, k.shape, 1)
    blk_k = lax.broadcasted_iota(jnp.int32, k.shape, 0) // MOBA_BLOCK
    qi = lax.broadcasted_iota(jnp.int32, (MOBA_BLOCK, MOBA_BLOCK), 0)
    kj = lax.broadcasted_iota(jnp.int32, (MOBA_BLOCK, MOBA_BLOCK), 1)
    rel = qi - kj
    for hh in range(2):
        aug = (1 - hh) * MOBA_HD
        in_head_k = (lane_k >= hh * MOBA_HD) & (lane_k < (hh + 1) * MOBA_HD)
        ka_ref[hh] = jnp.where(in_head_k, k, (lane_k - aug == blk_k).astype(F32)).astype(BF16)
        b_far = bias_ref[2 * hp + hh, N_BUCKETS - 1]
        own = _rel_bias_tile(bias_ref, 2 * hp + hh, jnp.maximum(rel, 0)) - b_far
        bt_ref[hh, 0] = jnp.where(rel >= 0, own, -jnp.inf)
        bt_ref[hh, 1] = _rel_bias_tile(bias_ref, 2 * hp + hh, rel + MOBA_BLOCK) - b_far
    lane = lax.broadcasted_iota(jnp.int32, (MOBA_BLOCK, LANES), 1)
    blk_t = lax.broadcasted_iota(jnp.int32, (nb, MOBA_BLOCK), 0)

    def lane_tile_max(s):
        parts = [s[:, i * LANES:(i + 1) * LANES] for i in range(s.shape[1] // LANES)]
        return functools.reduce(jnp.maximum, parts)

    def lane_tile_sum(s):
        parts = [s[:, i * LANES:(i + 1) * LANES] for i in range(s.shape[1] // LANES)]
        return functools.reduce(jnp.add, parts)

    def q_block(qb, carry):
        rows = pl.ds(pl.multiple_of(qb * MOBA_BLOCK, MOBA_BLOCK), MOBA_BLOCK)
        q2 = q_ref[rows, :] * (MOBA_HD ** -0.5)
        own_rows = rows
        near_rows = pl.ds(pl.multiple_of(jnp.maximum(qb - 1, 0) * MOBA_BLOCK, MOBA_BLOCK), MOBA_BLOCK)
        npair = qb // 2
        q_far, mts = [], []
        for hh in range(2):
            aug = (1 - hh) * MOBA_HD
            in_head = (lane >= hh * MOBA_HD) & (lane < (hh + 1) * MOBA_HD)
            qh = jnp.where(in_head, q2, 0.0)
            sc = jnp.where(blk_t < qb, _mm_nt(kmean, qh, HIGHEST), -jnp.inf)
            sel = jnp.zeros(sc.shape, F32)
            for _ in range(MOBA_TOPK):
                m = jnp.max(sc, axis=0, keepdims=True)
                cand = jnp.where((sc == m) & (sc > -jnp.inf), blk_t, nb)
                pick = blk_t == jnp.min(cand, axis=0, keepdims=True)
                sel = jnp.where(pick, 1.0, sel)
                sc = jnp.where(pick, -jnp.inf, sc)
            pen_near = jnp.where(sel > 0.0, 0.0, MOBA_NEG)
            pen_far = jnp.where(blk_t == qb - 1, MOBA_NEG, pen_near)

            def place(pen_t):
                parts = [jnp.zeros((aug, MOBA_BLOCK), F32)] if aug else []
                parts += [pen_t, jnp.zeros((LANES - aug - nb, MOBA_BLOCK), F32)]
                return jnp.concatenate(parts, axis=0).T

            q_own = qh.astype(BF16)
            q_near = jnp.where(in_head, q2, place(pen_near)).astype(BF16)
            q_far.append(jnp.where(in_head, q2, place(pen_far)).astype(BF16))
            s_own = _mm_nt(q_own, ka_ref[hh, own_rows, :]) + bt_ref[hh, 0]
            s_near = _mm_nt(q_near, ka_ref[hh, near_rows, :]) + bt_ref[hh, 1]
            s_ref[hh, 0, :, 0:MOBA_BLOCK] = s_own
            s_ref[hh, 0, :, MOBA_BLOCK:MOBA_PAIR] = s_near
            mts.append(jnp.maximum(lane_tile_max(s_own), lane_tile_max(s_near)))

        def far_scores(p, mt):
            ks = pl.ds(pl.multiple_of(p * MOBA_PAIR, MOBA_PAIR), MOBA_PAIR)
            out = []
            for hh in range(2):
                s = _mm_nt(q_far[hh], ka_ref[hh, ks, :])
                s_ref[hh, 1 + p] = s
                out.append(jnp.maximum(mt[hh], lane_tile_max(s)))
            return tuple(out)

        mts = lax.fori_loop(0, npair, far_scores, tuple(mts))
        ms = [jnp.max(mt, axis=-1, keepdims=True) for mt in mts]

        lts, accs = [], []
        for hh in range(2):
            p = jnp.exp(s_ref[hh, 0] - ms[hh])
            p16 = p.astype(BF16)
            lts.append(lane_tile_sum(p))
            accs.append(_mm(p16[:, 0:MOBA_BLOCK], v16_ref[own_rows, :])
                        + _mm(p16[:, MOBA_BLOCK:MOBA_PAIR], v16_ref[near_rows, :]))

        def far_values(p, st):
            ks = pl.ds(pl.multiple_of(p * MOBA_PAIR, MOBA_PAIR), MOBA_PAIR)
            lt, acc = st
            v_pair = v16_ref[ks, :]
            lt_new, acc_new = [], []
            for hh in range(2):
                pp = jnp.exp(s_ref[hh, 1 + p] - ms[hh])
                lt_new.append(lt[hh] + lane_tile_sum(pp))
                acc_new.append(acc[hh] + _mm(pp.astype(BF16), v_pair))
            return tuple(lt_new), tuple(acc_new)

        lts, accs = lax.fori_loop(0, npair, far_values, (tuple(lts), tuple(accs)))
        outs = [accs[hh] / jnp.sum(lts[hh], axis=-1, keepdims=True) for hh in range(2)]
        o_ref[rows, :] = jnp.where(lane < MOBA_HD, outs[0], outs[1]).astype(o_ref.dtype)
        return carry

    lax.fori_loop(0, nb, q_block, 0)


def _moba(proj, rel_bias, batch, seq):
    npair = MOBA_HEADS // 2
    qb, kb, vb = ((C_MB + i * MOBA_DIM) // LANES for i in range(3))
    blk = lambda base: pl.BlockSpec((seq, LANES), lambda b, h: (b, base + h))
    return pl.pallas_call(
        functools.partial(_moba_kernel, seq=seq),
        grid=(batch, npair),
        in_specs=[pl.BlockSpec(memory_space=pltpu.SMEM), blk(qb), blk(kb), blk(vb)],
        out_specs=pl.BlockSpec((seq, LANES), lambda b, h: (b, h)),
        out_shape=jax.ShapeDtypeStruct((batch * seq, MOBA_DIM), BF16),
        scratch_shapes=[pltpu.VMEM((2, seq, LANES), BF16), pltpu.VMEM((seq, LANES), BF16),
                        pltpu.VMEM((2, 2, MOBA_BLOCK, MOBA_BLOCK), F32),
                        pltpu.VMEM((2, 1 + seq // MOBA_PAIR, MOBA_BLOCK, MOBA_PAIR), F32)],
        compiler_params=_cparams(("parallel", "parallel")),
        name="moba",
    )(rel_bias, proj, proj, proj)


def _mix_kernel(oa_ref, ob_ref, ga_ref, gb_ref, x_ref, wa_ref, wb_ref, wo_ref, g_ref, b_ref, o_ref):
    ya = _mm(oa_ref[...], wa_ref[...])
    yb = _mm(ob_ref[...], wb_ref[...])
    m = jax.nn.sigmoid(ga_ref[...]) * ya + jax.nn.sigmoid(gb_ref[...]) * yb
    mix = _mm(m.astype(BF16), wo_ref[...])
    o_ref[...] = _layer_norm(DEEPNORM_ALPHA * x_ref[...] + mix, g_ref[...], b_ref[...])


def _mix(oa, ob, proj, x, wa, wb, wo, g, b, tm=512):
    t, d = x.shape
    row = lambda col: (lambda i: (i, col))
    full = lambda a: pl.BlockSpec(a.shape, lambda i: (0, 0))
    return pl.pallas_call(
        _mix_kernel,
        grid=(t // tm,),
        in_specs=[pl.BlockSpec((tm, V_DIM), row(0)), pl.BlockSpec((tm, MOBA_DIM), row(0)),
                  pl.BlockSpec((tm, d), row(C_GATE // d)), pl.BlockSpec((tm, d), row(C_GATE // d + 1)),
                  pl.BlockSpec((tm, d), row(0)), full(wa), full(wb), full(wo), full(g), full(b)],
        out_specs=pl.BlockSpec((tm, d), row(0)),
        out_shape=jax.ShapeDtypeStruct((t, d), F32),
        compiler_params=_cparams(("parallel",)),
        name="mix",
    )(oa, ob, proj, proj, x, wa, wb, wo, g, b)


ROUTE_IDX = 0
ROUTE_W = 2
MOE_TOPK = 2
MOE_TM = 512


def _router_kernel(x_ref, w_ref, b_ref, o_ref):
    logits = _mm(x_ref[...], w_ref[...], HIGHEST) + b_ref[...]
    lane = lax.broadcasted_iota(jnp.int32, logits.shape, 1)
    sc = jnp.where(lane < N_EXPERTS, logits, -jnp.inf)
    m1 = jnp.max(sc, axis=-1, keepdims=True)
    i1 = jnp.min(jnp.where(sc == m1, lane, LANES), axis=-1, keepdims=True)
    sc2 = jnp.where(lane == i1, -jnp.inf, sc)
    m2 = jnp.max(sc2, axis=-1, keepdims=True)
    i2 = jnp.min(jnp.where(sc2 == m2, lane, LANES), axis=-1, keepdims=True)
    e2 = jnp.exp(m2 - m1)
    den = 1.0 + e2
    out = jnp.where(lane == ROUTE_IDX, i1.astype(F32), jnp.where(lane == ROUTE_IDX + 1, i2.astype(F32), 0.0))
    o_ref[...] = jnp.where(lane == ROUTE_W, 1.0 / den, jnp.where(lane == ROUTE_W + 1, e2 / den, out))


def _router(x, w, b, tm=512):
    t, d = x.shape
    return pl.pallas_call(
        _router_kernel,
        grid=(t // tm,),
        in_specs=[pl.BlockSpec((tm, d), lambda i: (i, 0)),
                  pl.BlockSpec((d, LANES), lambda i: (0, 0)),
                  pl.BlockSpec((1, LANES), lambda i: (0, 0))],
        out_specs=pl.BlockSpec((tm, LANES), lambda i: (i, 0)),
        out_shape=jax.ShapeDtypeStruct((t, LANES), F32),
        compiler_params=_cparams(("parallel",)),
        name="router",
    )(x, w, b)


def _swiglu_step(x16_ref, wg_ref, wu_ref, wd_ref, acc_ref):
    x16 = x16_ref[...]
    hg = _mm(x16, wg_ref[0])
    hu = _mm(x16, wu_ref[0])
    h = hg * jax.nn.sigmoid(hg) * hu
    acc_ref[...] += _mm(h.astype(BF16), wd_ref[0])


def _ffn_kernel(x_ref, wg_ref, wu_ref, wd_ref, g_ref, b_ref, o_ref, acc_ref, x16_ref):
    j = pl.program_id(1)

    @pl.when(j == 0)
    def _():
        acc_ref[...] = jnp.zeros_like(acc_ref)
        x16_ref[...] = x_ref[...].astype(BF16)

    _swiglu_step(x16_ref, wg_ref, wu_ref, wd_ref, acc_ref)

    @pl.when(j == pl.num_programs(1) - 1)
    def _():
        o_ref[...] = _layer_norm(DEEPNORM_ALPHA * x_ref[...] + acc_ref[...], g_ref[...], b_ref[...])


def _ffn(x, wg, wu, wd, g, b, tm, tf):
    t, d = x.shape
    f = wg.shape[-1]
    return pl.pallas_call(
        _ffn_kernel,
        grid=(t // tm, f // tf),
        in_specs=[pl.BlockSpec((tm, d), lambda i, j: (i, 0)),
                  pl.BlockSpec((1, d, tf), lambda i, j: (0, 0, j)),
                  pl.BlockSpec((1, d, tf), lambda i, j: (0, 0, j)),
                  pl.BlockSpec((1, tf, d), lambda i, j: (0, j, 0)),
                  pl.BlockSpec((1, d), lambda i, j: (0, 0)),
                  pl.BlockSpec((1, d), lambda i, j: (0, 0))],
        out_specs=pl.BlockSpec((tm, d), lambda i, j: (i, 0)),
        out_shape=jax.ShapeDtypeStruct((t, d), F32),
        scratch_shapes=[pltpu.VMEM((tm, d), F32), pltpu.VMEM((tm, d), BF16)],
        compiler_params=_cparams(("parallel", "arbitrary")),
        name="ffn",
    )(x, wg, wu, wd, g, b)


def _row_copy(src_hbm, src_row, dst_ref, dst_row, sem):
    return pltpu.make_async_copy(src_hbm.at[pl.ds(src_row, 1), :], dst_ref.at[pl.ds(dst_row, 1), :], sem)


def _dispatch_kernel(pos_ref, x_hbm, init_hbm, xs_hbm, sem, *, rows):
    del init_hbm
    base = pl.program_id(0) * rows

    def issue(r, c):
        t = base + r
        for k in range(MOE_TOPK):
            _row_copy(x_hbm, t, xs_hbm, pos_ref[MOE_TOPK * t + k], sem).start()
        return c

    def drain(r, c):
        for k in range(MOE_TOPK):
            _row_copy(x_hbm, 0, xs_hbm, 0, sem).wait()
        return c

    lax.fori_loop(0, rows, issue, 0)
    lax.fori_loop(0, rows, drain, 0)


def _dispatch(pos, x, n_rows, rows=1024):
    t, d = x.shape
    rows = min(rows, t)
    return pl.pallas_call(
        functools.partial(_dispatch_kernel, rows=rows),
        grid_spec=pltpu.PrefetchScalarGridSpec(
            num_scalar_prefetch=1,
            grid=(t // rows,),
            in_specs=[pl.BlockSpec(memory_space=pl.ANY), pl.BlockSpec(memory_space=pl.ANY)],
            out_specs=pl.BlockSpec(memory_space=pl.ANY),
            scratch_shapes=[pltpu.SemaphoreType.DMA(())]),
        out_shape=jax.ShapeDtypeStruct((n_rows, d), F32),
        input_output_aliases={2: 0},
        compiler_params=_cparams(("arbitrary",)),
        name="moe_dispatch",
    )(pos, x, jnp.zeros((n_rows, d), F32))


def _moe_kernel(te_ref, tv_ref, x_ref, wg_ref, wu_ref, wd_ref, o_ref, acc_ref, x16_ref):
    del te_ref
    i = pl.program_id(0)
    j = pl.program_id(1)
    last = j == pl.num_programs(1) - 1
    used = tv_ref[i] != 0

    @pl.when(jnp.logical_and(used, j == 0))
    def _():
        acc_ref[...] = jnp.zeros_like(acc_ref)
        x16_ref[...] = x_ref[...].astype(BF16)

    @pl.when(used)
    def _():
        _swiglu_step(x16_ref, wg_ref, wu_ref, wd_ref, acc_ref)

    @pl.when(jnp.logical_and(used, last))
    def _():
        o_ref[...] = acc_ref[...]

    @pl.when(jnp.logical_and(jnp.logical_not(used), last))
    def _():
        o_ref[...] = jnp.zeros_like(o_ref)


def _moe_experts(tile_expert, tile_used, xs, wg, wu, wd, tf):
    p, d = xs.shape
    f = wg.shape[-1]
    nj = f // tf
    col = lambda i, j, te, tv: jnp.where(tv[i] != 0, j, nj - 1)
    return pl.pallas_call(
        _moe_kernel,
        grid_spec=pltpu.PrefetchScalarGridSpec(
            num_scalar_prefetch=2,
            grid=(p // MOE_TM, nj),
            in_specs=[pl.BlockSpec((MOE_TM, d), lambda i, j, te, tv: (i, 0)),
                      pl.BlockSpec((1, d, tf), lambda i, j, te, tv: (te[i], 0, col(i, j, te, tv))),
                      pl.BlockSpec((1, d, tf), lambda i, j, te, tv: (te[i], 0, col(i, j, te, tv))),
                      pl.BlockSpec((1, tf, d), lambda i, j, te, tv: (te[i], col(i, j, te, tv), 0))],
            out_specs=pl.BlockSpec((MOE_TM, d), lambda i, j, te, tv: (i, 0)),
            scratch_shapes=[pltpu.VMEM((MOE_TM, d), F32), pltpu.VMEM((MOE_TM, d), BF16)]),
        out_shape=jax.ShapeDtypeStruct((p, d), F32),
        compiler_params=_cparams(("parallel", "arbitrary")),
        name="moe_experts",
    )(tile_expert, tile_used, xs, wg, wu, wd)


def _combine_kernel(pos_ref, x_ref, rt_ref, ys_hbm, g_ref, b_ref, o_ref, buf_ref, sem, *, tm):
    base = pl.program_id(0) * tm

    def issue(r, c):
        for k in range(MOE_TOPK):
            _row_copy(ys_hbm, pos_ref[MOE_TOPK * (base + r) + k], buf_ref.at[k], r, sem).start()
        return c

    def drain(r, c):
        for k in range(MOE_TOPK):
            _row_copy(ys_hbm, 0, buf_ref.at[k], r, sem).wait()
        return c

    lax.fori_loop(0, tm, issue, 0)
    lax.fori_loop(0, tm, drain, 0)
    rt = rt_ref[...]
    y = rt[:, ROUTE_W:ROUTE_W + 1] * buf_ref[0] + rt[:, ROUTE_W + 1:ROUTE_W + 2] * buf_ref[1]
    o_ref[...] = _layer_norm(DEEPNORM_ALPHA * x_ref[...] + y, g_ref[...], b_ref[...])


def _combine(pos, x, route, ys, g, b, tm=512):
    t, d = x.shape
    tm = min(tm, t)
    return pl.pallas_call(
        functools.partial(_combine_kernel, tm=tm),
        grid_spec=pltpu.PrefetchScalarGridSpec(
            num_scalar_prefetch=1,
            grid=(t // tm,),
            in_specs=[pl.BlockSpec((tm, d), lambda i, pos: (i, 0)),
                      pl.BlockSpec((tm, LANES), lambda i, pos: (i, 0)),
                      pl.BlockSpec(memory_space=pl.ANY),
                      pl.BlockSpec((1, d), lambda i, pos: (0, 0)),
                      pl.BlockSpec((1, d), lambda i, pos: (0, 0))],
            out_specs=pl.BlockSpec((tm, d), lambda i, pos: (i, 0)),
            scratch_shapes=[pltpu.VMEM((MOE_TOPK, tm, d), F32), pltpu.SemaphoreType.DMA(())]),
        out_shape=jax.ShapeDtypeStruct((t, d), F32),
        compiler_params=_cparams(("arbitrary",)),
        name="moe_combine",
    )(pos, x, route, ys, g, b)


def _moe(x, route, wg, wu, wd, g, b, tf=512):
    t, d = x.shape
    n_assign = MOE_TOPK * t
    n_rows = n_assign + N_EXPERTS * MOE_TM
    n_tiles = n_rows // MOE_TM
    expert = route[:, ROUTE_IDX:ROUTE_IDX + MOE_TOPK].astype(jnp.int32).reshape(n_assign)
    onehot = (expert[:, None] == jnp.arange(N_EXPERTS, dtype=jnp.int32)[None, :]).astype(jnp.int32)
    rank = jnp.sum((jnp.cumsum(onehot, axis=0) - onehot) * onehot, axis=1)
    count = jnp.sum(onehot, axis=0)
    padded = (count + MOE_TM - 1) // MOE_TM * MOE_TM
    group_end = jnp.cumsum(padded)
    group_start = group_end - padded
    pos = (jnp.sum(onehot * group_start[None, :], axis=1) + rank).astype(jnp.int32)
    tile_start = jnp.arange(n_tiles, dtype=jnp.int32) * MOE_TM
    tile_used = (tile_start < group_end[-1]).astype(jnp.int32)
    tile_expert = jnp.sum((tile_start[:, None] >= group_end[None, :]).astype(jnp.int32), axis=1)
    last_expert = jnp.sum((group_end[-1] - 1 >= group_end).astype(jnp.int32))
    tile_expert = jnp.where(tile_used != 0, tile_expert, last_expert).astype(jnp.int32)
    xs = _dispatch(pos, x, n_rows)
    ys = _moe_experts(tile_expert, tile_used, xs, wg, wu, wd, tf)
    return _combine(pos, x, route, ys, g, b)


def _pad_lanes(v):
    v = v.reshape(1, -1).astype(F32)
    return jnp.pad(v, ((0, 0), (0, LANES - v.shape[1])))


def kernel(x, w_in, conv_w, a_log, dt_bias, dn_norm_w, w_up_a, w_up_b, w_o, rel_bias, ln1_g, ln1_b,
           ln2_g, ln2_b, ffn_w_gate, ffn_w_up, ffn_w_down, router_w, router_b, exp_w_gate, exp_w_up,
           exp_w_down):
    batch, seq, d = x.shape
    t = batch * seq
    assert d == D_MODEL and seq % MOBA_BLOCK == 0 and seq // MOBA_BLOCK <= LANES
    c0 = N_QKV_DN
    c1 = c0 + V_DIM
    c2 = c1 + DN_HEADS
    c3 = c2 + DN_HEADS
    c4 = c3 + 3 * MOBA_DIM
    xf = x.reshape(t, d)
    for layer in range(DEPTH):
        w = w_in[layer]
        w_main = jnp.concatenate([w[:, :c1], w[:, c4:], w[:, c3:c4]], axis=1).astype(BF16)
        w_ab = jnp.pad(w[:, c1:c3], ((0, 0), (0, LANES - 2 * DN_HEADS))).astype(BF16)
        proj = _inproj(xf, w_main)
        gb = _gates(xf, w_ab, _pad_lanes(a_log[layer]), _pad_lanes(dt_bias[layer]))
        dct = gb[:, :DN_HEADS].reshape(t // DN_CHUNK, DN_CHUNK, DN_HEADS).transpose(0, 2, 1)
        qkv = _conv_prep(proj, conv_w[layer], batch, seq)
        o_a = _deltanet(qkv, proj, gb, dct, dn_norm_w[layer].reshape(1, DN_DV), batch, seq)
        o_b = _moba(proj, rel_bias, batch, seq)
        xf = _mix(o_a, o_b, proj, xf, w_up_a[layer].astype(BF16), w_up_b[layer].astype(BF16),
                  w_o[layer].astype(BF16), ln1_g[layer].reshape(1, d), ln1_b[layer].reshape(1, d))
        i = layer // 2
        g2 = ln2_g[layer].reshape(1, d)
        b2 = ln2_b[layer].reshape(1, d)
        if layer % 2 == 0:
            xf = _ffn(xf, ffn_w_gate[i][None].astype(BF16), ffn_w_up[i][None].astype(BF16),
                      ffn_w_down[i][None].astype(BF16), g2, b2, tm=512, tf=1408)
        else:
            rw = jnp.pad(router_w[i], ((0, 0), (0, LANES - N_EXPERTS)))
            route = _router(xf, rw, _pad_lanes(router_b[i]))
            xf = _moe(xf, route, exp_w_gate[i].astype(BF16), exp_w_up[i].astype(BF16),
                      exp_w_down[i].astype(BF16), g2, b2)
    return xf.reshape(batch, seq, d)
```

```python
import functools
import math

import numpy as np
import jax
import jax.numpy as jnp
from jax import lax
from jax.experimental import pallas as pl
from jax.experimental.pallas import tpu as pltpu

F32 = jnp.float32
BF16 = jnp.bfloat16
HIGHEST = lax.Precision.HIGHEST

D_MODEL = 1024
DEPTH = 4
DN_HEADS = 8
DN_DK = 128
DN_DV = 128
DN_CHUNK = 64
CONV_WIDTH = 4
QK_DIM = DN_HEADS * DN_DK
V_DIM = DN_HEADS * DN_DV
N_QKV_DN = 2 * QK_DIM + V_DIM
MOBA_HEADS = 8
MOBA_HD = 64
MOBA_DIM = MOBA_HEADS * MOBA_HD
MOBA_BLOCK = 256
MOBA_TOPK = 3
MOBA_Q_CHUNK = 128
N_BUCKETS = 32
REL_MAX_DIST = 128
N_EXPERTS = 8
DEEPNORM_ALPHA = (2 * DEPTH) ** 0.25
LN_EPS = 1e-5
RMS_EPS = 1e-6

LANES = 128
VMEM_LIMIT = 48 * 1024 * 1024

C_QKV = 0
C_Z = N_QKV_DN
C_GATE = C_Z + V_DIM
C_MB = C_GATE + 2 * D_MODEL
N_PROJ = C_MB + 3 * MOBA_DIM


def _bucket_starts():
    n = np.arange(0, 4 * REL_MAX_DIST)
    max_exact = N_BUCKETS // 2
    t = np.log(np.maximum(n, 1) / max_exact) / math.log(REL_MAX_DIST / max_exact) * (N_BUCKETS - max_exact)
    large = np.minimum(max_exact + t.astype(np.int64), N_BUCKETS - 1)
    b = np.where(n < max_exact, n, large)
    assert np.all(np.diff(b) >= 0) and b[-1] == N_BUCKETS - 1
    return [int(np.argmax(b >= k)) for k in range(N_BUCKETS)]


BUCKET_STARTS = _bucket_starts()
assert BUCKET_STARTS[-1] <= REL_MAX_DIST


def _cparams(sem):
    return pltpu.CompilerParams(dimension_semantics=sem, vmem_limit_bytes=VMEM_LIMIT)


def _mm(a, b, prec=None):
    return jnp.dot(a, b, preferred_element_type=F32, precision=prec)


def _mm_nt(a, b, prec=None):
    return lax.dot_general(a, b, (((1,), (1,)), ((), ())), preferred_element_type=F32, precision=prec)


def _mm_tn(a, b, prec=None):
    return lax.dot_general(a, b, (((0,), (0,)), ((), ())), preferred_element_type=F32, precision=prec)


def _bmm(a, b, prec=None):
    return lax.dot_general(a, b, (((2,), (1,)), ((0,), (0,))), preferred_element_type=F32, precision=prec)


def _bmm_nt(a, b, prec=None):
    return lax.dot_general(a, b, (((2,), (2,)), ((0,), (0,))), preferred_element_type=F32, precision=prec)


def _split(a):
    hi = a.astype(BF16)
    return hi, (a - hi.astype(F32)).astype(BF16)


def _bmm_split(a, b):
    return _bmm(a[0], b[0]) + (_bmm(a[0], b[1]) + _bmm(a[1], b[0]))


def _layer_norm(y, g, b):
    mu = jnp.mean(y, axis=-1, keepdims=True)
    yc = y - mu
    var = jnp.mean(yc * yc, axis=-1, keepdims=True)
    return yc * lax.rsqrt(var + LN_EPS) * g + b


def _inproj_kernel(x_ref, w_ref, o_ref, x16_ref):
    @pl.when(pl.program_id(1) == 0)
    def _():
        x16_ref[...] = x_ref[...].astype(BF16)

    o_ref[...] = _mm(x16_ref[...], w_ref[...])


def _inproj(x, w, tm=1024, tn=768):
    t, d = x.shape
    n = w.shape[1]
    tm = min(tm, t)
    return pl.pallas_call(
        _inproj_kernel,
        grid=(t // tm, n // tn),
        in_specs=[pl.BlockSpec((tm, d), lambda i, j: (i, 0)),
                  pl.BlockSpec((d, tn), lambda i, j: (0, j))],
        out_specs=pl.BlockSpec((tm, tn), lambda i, j: (i, j)),
        out_shape=jax.ShapeDtypeStruct((t, n), F32),
        scratch_shapes=[pltpu.VMEM((tm, d), BF16)],
        compiler_params=_cparams(("parallel", "arbitrary")),
        name="inproj",
    )(x, w)


def _gates_kernel(x_ref, w_ref, alog_ref, dtb_ref, o_ref):
    p = _mm(x_ref[...].astype(BF16), w_ref[...])
    z = p + dtb_ref[...]
    softplus = jnp.maximum(z, 0.0) + jnp.log1p(jnp.exp(-jnp.abs(z)))
    g = -jnp.exp(alog_ref[...]) * softplus
    tm = p.shape[0]
    r = lax.broadcasted_iota(jnp.int32, (tm, tm), 0)
    c = lax.broadcasted_iota(jnp.int32, (tm, tm), 1)
    tri = ((r // DN_CHUNK == c // DN_CHUNK) & (r >= c)).astype(F32)
    dc = _mm(tri, g, HIGHEST)
    lane = lax.broadcasted_iota(jnp.int32, p.shape, 1)
    o_ref[...] = jnp.where(lane < DN_HEADS, dc, jax.nn.sigmoid(p))


def _gates(x, w_ab, alog, dtb, tm=256):
    t, d = x.shape
    return pl.pallas_call(
        _gates_kernel,
        grid=(t // tm,),
        in_specs=[pl.BlockSpec((tm, d), lambda i: (i, 0)),
                  pl.BlockSpec((d, LANES), lambda i: (0, 0)),
                  pl.BlockSpec((1, LANES), lambda i: (0, 0)),
                  pl.BlockSpec((1, LANES), lambda i: (0, 0))],
        out_specs=pl.BlockSpec((tm, LANES), lambda i: (i, 0)),
        out_shape=jax.ShapeDtypeStruct((t, LANES), F32),
        compiler_params=_cparams(("parallel",)),
        name="gates",
    )(x, w_ab, alog, dtb)


def _conv_kernel(x_ref, w_ref, o_ref):
    j = pl.program_id(1)
    x = x_ref[...]
    w = w_ref[...]
    row = lax.broadcasted_iota(jnp.int32, x.shape, 0)
    y = x * w[CONV_WIDTH - 1:CONV_WIDTH, :]
    for k in range(1, CONV_WIDTH):
        xs = jnp.where(row >= k, pltpu.roll(x, k, axis=0), 0.0)
        y = y + xs * w[CONV_WIDTH - 1 - k:CONV_WIDTH - k, :]
    y = y * jax.nn.sigmoid(y)
    yn = y * lax.rsqrt(jnp.sum(y * y, axis=-1, keepdims=True) + RMS_EPS)
    scale = jnp.where(j < DN_HEADS, DN_DK ** -0.5, 1.0)
    o_ref[...] = jnp.where(j < 2 * DN_HEADS, yn * scale, y)


def _conv_prep(proj, conv_w, batch, seq):
    nblk = N_QKV_DN // LANES
    return pl.pallas_call(
        _conv_kernel,
        grid=(batch, nblk),
        in_specs=[pl.BlockSpec((seq, LANES), lambda b, j: (b, j)),
                  pl.BlockSpec((CONV_WIDTH, LANES), lambda b, j: (0, j))],
        out_specs=pl.BlockSpec((seq, LANES), lambda b, j: (b, j)),
        out_shape=jax.ShapeDtypeStruct((batch * seq, N_QKV_DN), F32),
        compiler_params=_cparams(("parallel", "parallel")),
        name="conv_prep",
    )(proj, conv_w)


def _dn_kernel(q_ref, k_ref, v_ref, z_ref, gb_ref, dct_ref, nw_ref, o_ref, s_ref, *, cb):
    @pl.when(pl.program_id(1) == 0)
    def _():
        s_ref[...] = jnp.zeros_like(s_ref)

    r = lax.broadcasted_iota(jnp.int32, (DN_CHUNK, DN_CHUNK), 0)
    c = lax.broadcasted_iota(jnp.int32, (DN_CHUNK, DN_CHUNK), 1)
    lower = r >= c
    strict = r > c
    eye = (r == c).astype(F32)
    nw = nw_ref[...]

    heads = range(DN_HEADS)
    head_cols = [slice(h * DN_DK, (h + 1) * DN_DK) for h in heads]

    chunk_rows = [slice(ci * DN_CHUNK, (ci + 1) * DN_CHUNK) for ci in range(cb)]
    units = [(ci, h) for ci in range(cb) for h in heads]
    gb = gb_ref[...]
    q = jnp.stack([q_ref[chunk_rows[ci], head_cols[h]] for ci, h in units])
    k = jnp.stack([k_ref[chunk_rows[ci], head_cols[h]] for ci, h in units])
    v = jnp.stack([v_ref[chunk_rows[ci], head_cols[h]] for ci, h in units])
    dc = jnp.stack([gb[chunk_rows[ci], h:h + 1] for ci, h in units])
    beta = jnp.stack([gb[chunk_rows[ci], DN_HEADS + h:DN_HEADS + h + 1] for ci, h in units])
    dcr = dct_ref[...].reshape(cb * DN_HEADS, 1, DN_CHUNK)
    dc_last = dc[:, DN_CHUNK - 1:DN_CHUNK, :]
    e_col = jnp.exp(dc)
    dm = jnp.exp(jnp.where(lower[None], dc - dcr, -jnp.inf))
    kb = k * beta
    vb = v * beta
    k16 = k.astype(BF16)
    kk = _bmm_nt(kb.astype(BF16), k16)
    nmat = jnp.where(strict[None], kk * dm, 0.0)
    pw = _split(-nmat)
    tinv = eye[None] - nmat
    for _ in range(5):
        pw = _split(_bmm_split(pw, pw))
        tinv = tinv + _bmm_split(_split(tinv), pw)
    sol = _bmm_split(_split(tinv), _split(jnp.concatenate([vb, kb * e_col], axis=-1)))
    value = sol[:, :, :DN_DV]
    att16 = (_bmm_nt(q.astype(BF16), k16) * dm).astype(BF16)
    kq16 = jnp.concatenate([sol[:, :, DN_DV:], q * e_col], axis=1).astype(BF16)
    kd16 = (k * jnp.exp(dc_last - dc)).astype(BF16)
    decay = jnp.exp(dc_last)

    s = s_ref[...]
    for ci in range(cb):
        us = slice(ci * DN_HEADS, (ci + 1) * DN_HEADS)
        rs = _bmm(kq16[us], s.astype(BF16))
        vn16 = (value[us] - rs[:, :DN_CHUNK]).astype(BF16)
        o = rs[:, DN_CHUNK:] + _bmm(att16[us], vn16)
        upd = jnp.stack([_mm_tn(kd16[ci * DN_HEADS + h], vn16[h]) for h in heads])
        s = s * decay[us] + upd
        o = o * lax.rsqrt(jnp.mean(o * o, axis=-1, keepdims=True) + RMS_EPS) * nw
        for h in heads:
            zz = z_ref[chunk_rows[ci], head_cols[h]]
            o_ref[chunk_rows[ci], head_cols[h]] = (o[h] * (zz * jax.nn.sigmoid(zz))).astype(o_ref.dtype)
    s_ref[...] = s


def _deltanet(qkv, proj, gb, dct, nw, batch, seq, cb=4):
    rows = cb * DN_CHUNK
    nblk = seq // rows
    t = batch * seq
    row_map = lambda col: (lambda b, i: (b * nblk + i, col))
    return pl.pallas_call(
        functools.partial(_dn_kernel, cb=cb),
        grid=(batch, nblk),
        in_specs=[pl.BlockSpec((rows, QK_DIM), row_map(0)),
                  pl.BlockSpec((rows, QK_DIM), row_map(1)),
                  pl.BlockSpec((rows, V_DIM), row_map(2)),
                  pl.BlockSpec((rows, V_DIM), row_map(C_Z // V_DIM)),
                  pl.BlockSpec((rows, LANES), row_map(0)),
                  pl.BlockSpec((cb, DN_HEADS, DN_CHUNK), lambda b, i: (b * nblk + i, 0, 0)),
                  pl.BlockSpec((1, DN_DV), lambda b, i: (0, 0))],
        out_specs=pl.BlockSpec((rows, V_DIM), row_map(0)),
        out_shape=jax.ShapeDtypeStruct((t, V_DIM), BF16),
        scratch_shapes=[pltpu.VMEM((DN_HEADS, DN_DK, DN_DV), F32)],
        compiler_params=_cparams(("parallel", "arbitrary")),
        name="deltanet",
    )(qkv, qkv, qkv, proj, gb, dct, nw)


def _rel_bias_tile(bias_ref, head, dist):
    val = jnp.full(dist.shape, bias_ref[head, 0], F32)
    for b in range(1, N_BUCKETS):
        val = jnp.where(dist >= BUCKET_STARTS[b], bias_ref[head, b], val)
    return val


MOBA_NEG = -1e30
MOBA_PAIR = 2 * MOBA_BLOCK


def _moba_kernel(bias_ref, q_ref, k_ref, v_ref, o_ref, ka_ref, v16_ref, bt_ref, s_ref, *, seq):
    hp = pl.program_id(1)
    nb = seq // MOBA_BLOCK
    k = k_ref[...]
    v16_ref[...] = v_ref[...].astype(BF16)
    kmean = jnp.mean(k.reshape(nb, MOBA_BLOCK, LANES), axis=1)
    lane_k = lax.broadcasted_iota(jnp.int32, k.shape, 1)
    blk_k = lax.broadcasted_iota(jnp.int32, k.shape, 0) // MOBA_BLOCK
    qi = lax.broadcasted_iota(jnp.int32, (MOBA_BLOCK, MOBA_BLOCK), 0)
    kj = lax.broadcasted_iota(jnp.int32, (MOBA_BLOCK, MOBA_BLOCK), 1)
    rel = qi - kj
    for hh in range(2):
        aug = (1 - hh) * MOBA_HD
        in_head_k = (lane_k >= hh * MOBA_HD) & (lane_k < (hh + 1) * MOBA_HD)
        ka_ref[hh] = jnp.where(in_head_k, k, (lane_k - aug == blk_k).astype(F32)).astype(BF16)
        b_far = bias_ref[2 * hp + hh, N_BUCKETS - 1]
        own = _rel_bias_tile(bias_ref, 2 * hp + hh, jnp.maximum(rel, 0)) - b_far
        bt_ref[hh, 0] = jnp.where(rel >= 0, own, -jnp.inf)
        bt_ref[hh, 1] = _rel_bias_tile(bias_ref, 2 * hp + hh, rel + MOBA_BLOCK) - b_far
    lane = lax.broadcasted_iota(jnp.int32, (MOBA_BLOCK, LANES), 1)
    blk_t = lax.broadcasted_iota(jnp.int32, (nb, MOBA_BLOCK), 0)

    def lane_tile_max(s):
        parts = [s[:, i * LANES:(i + 1) * LANES] for i in range(s.shape[1] // LANES)]
        return functools.reduce(jnp.maximum, parts)

    def lane_tile_sum(s):
        parts = [s[:, i * LANES:(i + 1) * LANES] for i in range(s.shape[1] // LANES)]
        return functools.reduce(jnp.add, parts)

    def q_block(qb, carry):
        rows = pl.ds(pl.multiple_of(qb * MOBA_BLOCK, MOBA_BLOCK), MOBA_BLOCK)
        q2 = q_ref[rows, :] * (MOBA_HD ** -0.5)
        own_rows = rows
        near_rows = pl.ds(pl.multiple_of(jnp.maximum(qb - 1, 0) * MOBA_BLOCK, MOBA_BLOCK), MOBA_BLOCK)
        npair = qb // 2
        q_far, mts = [], []
        for hh in range(2):
            aug = (1 - hh) * MOBA_HD
            in_head = (lane >= hh * MOBA_HD) & (lane < (hh + 1) * MOBA_HD)
            qh = jnp.where(in_head, q2, 0.0)
            sc = jnp.where(blk_t < qb, _mm_nt(kmean, qh, HIGHEST), -jnp.inf)
            sel = jnp.zeros(sc.shape, F32)
            for _ in range(MOBA_TOPK):
                m = jnp.max(sc, axis=0, keepdims=True)
                cand = jnp.where((sc == m) & (sc > -jnp.inf), blk_t, nb)
                pick = blk_t == jnp.min(cand, axis=0, keepdims=True)
                sel = jnp.where(pick, 1.0, sel)
                sc = jnp.where(pick, -jnp.inf, sc)
            pen_near = jnp.where(sel > 0.0, 0.0, MOBA_NEG)
            pen_far = jnp.where(blk_t == qb - 1, MOBA_NEG, pen_near)

            def place(pen_t):
                parts = [jnp.zeros((aug, MOBA_BLOCK), F32)] if aug else []
                parts += [pen_t, jnp.zeros((LANES - aug - nb, MOBA_BLOCK), F32)]
                return jnp.concatenate(parts, axis=0).T

            q_own = qh.astype(BF16)
            q_near = jnp.where(in_head, q2, place(pen_near)).astype(BF16)
            q_far.append(jnp.where(in_head, q2, place(pen_far)).astype(BF16))
            s_own = _mm_nt(q_own, ka_ref[hh, own_rows, :]) + bt_ref[hh, 0]
            s_near = _mm_nt(q_near, ka_ref[hh, near_rows, :]) + bt_ref[hh, 1]
            s_ref[hh, 0, :, 0:MOBA_BLOCK] = s_own
            s_ref[hh, 0, :, MOBA_BLOCK:MOBA_PAIR] = s_near
            mts.append(jnp.maximum(lane_tile_max(s_own), lane_tile_max(s_near)))

        def far_scores(p, mt):
            ks = pl.ds(pl.multiple_of(p * MOBA_PAIR, MOBA_PAIR), MOBA_PAIR)
            out = []
            for hh in range(2):
                s = _mm_nt(q_far[hh], ka_ref[hh, ks, :])
                s_ref[hh, 1 + p] = s
                out.append(jnp.maximum(mt[hh], lane_tile_max(s)))
            return tuple(out)

        mts = lax.fori_loop(0, npair, far_scores, tuple(mts))
        ms = [jnp.max(mt, axis=-1, keepdims=True) for mt in mts]

        lts, accs = [], []
        for hh in range(2):
            p = jnp.exp(s_ref[hh, 0] - ms[hh])
            p16 = p.astype(BF16)
            lts.append(lane_tile_sum(p))
            accs.append(_mm(p16[:, 0:MOBA_BLOCK], v16_ref[own_rows, :])
                        + _mm(p16[:, MOBA_BLOCK:MOBA_PAIR], v16_ref[near_rows, :]))

        def far_values(p, st):
            ks = pl.ds(pl.multiple_of(p * MOBA_PAIR, MOBA_PAIR), MOBA_PAIR)
            lt, acc = st
            v_pair = v16_ref[ks, :]
            lt_new, acc_new = [], []
            for hh in range(2):
                pp = jnp.exp(s_ref[hh, 1 + p] - ms[hh])
                lt_new.append(lt[hh] + lane_tile_sum(pp))
                acc_new.append(acc[hh] + _mm(pp.astype(BF16), v_pair))
            return tuple(lt_new), tuple(acc_new)

        lts, accs = lax.fori_loop(0, npair, far_values, (tuple(lts), tuple(accs)))
        outs = [accs[hh] / jnp.sum(lts[hh], axis=-1, keepdims=True) for hh in range(2)]
        o_ref[rows, :] = jnp.where(lane < MOBA_HD, outs[0], outs[1]).astype(o_ref.dtype)
        return carry

    lax.fori_loop(0, nb, q_block, 0)


def _moba(proj, rel_bias, batch, seq):
    npair = MOBA_HEADS // 2
    qb, kb, vb = ((C_MB + i * MOBA_DIM) // LANES for i in range(3))
    blk = lambda base: pl.BlockSpec((seq, LANES), lambda b, h: (b, base + h))
    return pl.pallas_call(
        functools.partial(_moba_kernel, seq=seq),
        grid=(batch, npair),
        in_specs=[pl.BlockSpec(memory_space=pltpu.SMEM), blk(qb), blk(kb), blk(vb)],
        out_specs=pl.BlockSpec((seq, LANES), lambda b, h: (b, h)),
        out_shape=jax.ShapeDtypeStruct((batch * seq, MOBA_DIM), BF16),
        scratch_shapes=[pltpu.VMEM((2, seq, LANES), BF16), pltpu.VMEM((seq, LANES), BF16),
                        pltpu.VMEM((2, 2, MOBA_BLOCK, MOBA_BLOCK), F32),
                        pltpu.VMEM((2, 1 + seq // MOBA_PAIR, MOBA_BLOCK, MOBA_PAIR), F32)],
        compiler_params=_cparams(("parallel", "parallel")),
        name="moba",
    )(rel_bias, proj, proj, proj)


def _mix_kernel(oa_ref, ob_ref, ga_ref, gb_ref, x_ref, wa_ref, wb_ref, wo_ref, g_ref, b_ref, o_ref):
    ya = _mm(oa_ref[...], wa_ref[...])
    yb = _mm(ob_ref[...], wb_ref[...])
    m = jax.nn.sigmoid(ga_ref[...]) * ya + jax.nn.sigmoid(gb_ref[...]) * yb
    mix = _mm(m.astype(BF16), wo_ref[...])
    o_ref[...] = _layer_norm(DEEPNORM_ALPHA * x_ref[...] + mix, g_ref[...], b_ref[...])


def _mix(oa, ob, proj, x, wa, wb, wo, g, b, tm=512):
    t, d = x.shape
    row = lambda col: (lambda i: (i, col))
    full = lambda a: pl.BlockSpec(a.shape, lambda i: (0, 0))
    return pl.pallas_call(
        _mix_kernel,
        grid=(t // tm,),
        in_specs=[pl.BlockSpec((tm, V_DIM), row(0)), pl.BlockSpec((tm, MOBA_DIM), row(0)),
                  pl.BlockSpec((tm, d), row(C_GATE // d)), pl.BlockSpec((tm, d), row(C_GATE // d + 1)),
                  pl.BlockSpec((tm, d), row(0)), full(wa), full(wb), full(wo), full(g), full(b)],
        out_specs=pl.BlockSpec((tm, d), row(0)),
        out_shape=jax.ShapeDtypeStruct((t, d), F32),
        compiler_params=_cparams(("parallel",)),
        name="mix",
    )(oa, ob, proj, proj, x, wa, wb, wo, g, b)


ROUTE_IDX = 0
ROUTE_W = 2
MOE_TOPK = 2
MOE_TM = 512


def _router_kernel(x_ref, w_ref, b_ref, o_ref):
    logits = _mm(x_ref[...], w_ref[...], HIGHEST) + b_ref[...]
    lane = lax.broadcasted_iota(jnp.int32, logits.shape, 1)
    sc = jnp.where(lane < N_EXPERTS, logits, -jnp.inf)
    m1 = jnp.max(sc, axis=-1, keepdims=True)
    i1 = jnp.min(jnp.where(sc == m1, lane, LANES), axis=-1, keepdims=True)
    sc2 = jnp.where(lane == i1, -jnp.inf, sc)
    m2 = jnp.max(sc2, axis=-1, keepdims=True)
    i2 = jnp.min(jnp.where(sc2 == m2, lane, LANES), axis=-1, keepdims=True)
    e2 = jnp.exp(m2 - m1)
    den = 1.0 + e2
    out = jnp.where(lane == ROUTE_IDX, i1.astype(F32), jnp.where(lane == ROUTE_IDX + 1, i2.astype(F32), 0.0))
    o_ref[...] = jnp.where(lane == ROUTE_W, 1.0 / den, jnp.where(lane == ROUTE_W + 1, e2 / den, out))


def _router(x, w, b, tm=512):
    t, d = x.shape
    return pl.pallas_call(
        _router_kernel,
        grid=(t // tm,),
        in_specs=[pl.BlockSpec((tm, d), lambda i: (i, 0)),
                  pl.BlockSpec((d, LANES), lambda i: (0, 0)),
                  pl.BlockSpec((1, LANES), lambda i: (0, 0))],
        out_specs=pl.BlockSpec((tm, LANES), lambda i: (i, 0)),
        out_shape=jax.ShapeDtypeStruct((t, LANES), F32),
        compiler_params=_cparams(("parallel",)),
        name="router",
    )(x, w, b)


def _swiglu_step(x16_ref, wg_ref, wu_ref, wd_ref, acc_ref):
    x16 = x16_ref[...]
    hg = _mm(x16, wg_ref[0])
    hu = _mm(x16, wu_ref[0])
    h = hg * jax.nn.sigmoid(hg) * hu
    acc_ref[...] += _mm(h.astype(BF16), wd_ref[0])


def _ffn_kernel(x_ref, wg_ref, wu_ref, wd_ref, g_ref, b_ref, o_ref, acc_ref, x16_ref):
    j = pl.program_id(1)

    @pl.when(j == 0)
    def _():
        acc_ref[...] = jnp.zeros_like(acc_ref)
        x16_ref[...] = x_ref[...].astype(BF16)

    _swiglu_step(x16_ref, wg_ref, wu_ref, wd_ref, acc_ref)

    @pl.when(j == pl.num_programs(1) - 1)
    def _():
        o_ref[...] = _layer_norm(DEEPNORM_ALPHA * x_ref[...] + acc_ref[...], g_ref[...], b_ref[...])


def _ffn(x, wg, wu, wd, g, b, tm, tf):
    t, d = x.shape
    f = wg.shape[-1]
    return pl.pallas_call(
        _ffn_kernel,
        grid=(t // tm, f // tf),
        in_specs=[pl.BlockSpec((tm, d), lambda i, j: (i, 0)),
                  pl.BlockSpec((1, d, tf), lambda i, j: (0, 0, j)),
                  pl.BlockSpec((1, d, tf), lambda i, j: (0, 0, j)),
                  pl.BlockSpec((1, tf, d), lambda i, j: (0, j, 0)),
                  pl.BlockSpec((1, d), lambda i, j: (0, 0)),
                  pl.BlockSpec((1, d), lambda i, j: (0, 0))],
        out_specs=pl.BlockSpec((tm, d), lambda i, j: (i, 0)),
        out_shape=jax.ShapeDtypeStruct((t, d), F32),
        scratch_shapes=[pltpu.VMEM((tm, d), F32), pltpu.VMEM((tm, d), BF16)],
        compiler_params=_cparams(("parallel", "arbitrary")),
        name="ffn",
    )(x, wg, wu, wd, g, b)


def _row_copy(src_hbm, src_row, dst_ref, dst_row, sem):
    return pltpu.make_async_copy(src_hbm.at[pl.ds(src_row, 1), :], dst_ref.at[pl.ds(dst_row, 1), :], sem)


def _dispatch_kernel(pos_ref, x_ref, init_hbm, xs_hbm, sem, *, rows):
    del init_hbm
    base = pl.program_id(0) * rows

    def issue(r, c):
        for k in range(MOE_TOPK):
            _row_copy(x_ref, r, xs_hbm, pos_ref[MOE_TOPK * (base + r) + k], sem).start()
        return c

    def drain(r, c):
        for k in range(MOE_TOPK):
            _row_copy(x_ref, 0, xs_hbm, 0, sem).wait()
        return c

    lax.fori_loop(0, rows, issue, 0)
    lax.fori_loop(0, rows, drain, 0)


def _dispatch(pos, x, n_rows, rows=512):
    t, d = x.shape
    rows = min(rows, t)
    return pl.pallas_call(
        functools.partial(_dispatch_kernel, rows=rows),
        grid_spec=pltpu.PrefetchScalarGridSpec(
            num_scalar_prefetch=1,
            grid=(t // rows,),
            in_specs=[pl.BlockSpec((rows, d), lambda i, pos: (i, 0)), pl.BlockSpec(memory_space=pl.ANY)],
            out_specs=pl.BlockSpec(memory_space=pl.ANY),
            scratch_shapes=[pltpu.SemaphoreType.DMA(())]),
        out_shape=jax.ShapeDtypeStruct((n_rows, d), F32),
        input_output_aliases={2: 0},
        compiler_params=_cparams(("arbitrary",)),
        name="moe_dispatch",
    )(pos, x, jnp.zeros((n_rows, d), F32))


def _moe_kernel(te_ref, tv_ref, x_ref, wg_ref, wu_ref, wd_ref, o_ref, acc_ref, x16_ref):
    del te_ref
    i = pl.program_id(0)
    j = pl.program_id(1)
    last = j == pl.num_programs(1) - 1
    used = tv_ref[i] != 0

    @pl.when(jnp.logical_and(used, j == 0))
    def _():
        acc_ref[...] = jnp.zeros_like(acc_ref)
        x16_ref[...] = x_ref[...].astype(BF16)

    @pl.when(used)
    def _():
        _swiglu_step(x16_ref, wg_ref, wu_ref, wd_ref, acc_ref)

    @pl.when(jnp.logical_and(used, last))
    def _():
        o_ref[...] = acc_ref[...]

    @pl.when(jnp.logical_and(jnp.logical_not(used), last))
    def _():
        o_ref[...] = jnp.zeros_like(o_ref)


def _moe_experts(tile_expert, tile_used, xs, wg, wu, wd, tf):
    p, d = xs.shape
    f = wg.shape[-1]
    nj = f // tf
    col = lambda i, j, te, tv: jnp.where(tv[i] != 0, j, nj - 1)
    return pl.pallas_call(
        _moe_kernel,
        grid_spec=pltpu.PrefetchScalarGridSpec(
            num_scalar_prefetch=2,
            grid=(p // MOE_TM, nj),
            in_specs=[pl.BlockSpec((MOE_TM, d), lambda i, j, te, tv: (i, 0)),
                      pl.BlockSpec((1, d, tf), lambda i, j, te, tv: (te[i], 0, col(i, j, te, tv))),
                      pl.BlockSpec((1, d, tf), lambda i, j, te, tv: (te[i], 0, col(i, j, te, tv))),
                      pl.BlockSpec((1, tf, d), lambda i, j, te, tv: (te[i], col(i, j, te, tv), 0))],
            out_specs=pl.BlockSpec((MOE_TM, d), lambda i, j, te, tv: (i, 0)),
            scratch_shapes=[pltpu.VMEM((MOE_TM, d), F32), pltpu.VMEM((MOE_TM, d), BF16)]),
        out_shape=jax.ShapeDtypeStruct((p, d), F32),
        compiler_params=_cparams(("parallel", "arbitrary")),
        name="moe_experts",
    )(tile_expert, tile_used, xs, wg, wu, wd)


def _combine_kernel(pos_ref, x_ref, rt_ref, ys_hbm, g_ref, b_ref, o_ref, buf_ref, sem, *, tm):
    base = pl.program_id(0) * tm

    def issue(r, c):
        for k in range(MOE_TOPK):
            _row_copy(ys_hbm, pos_ref[MOE_TOPK * (base + r) + k], buf_ref.at[k], r, sem).start()
        return c

    def drain(r, c):
        for k in range(MOE_TOPK):
            _row_copy(ys_hbm, 0, buf_ref.at[k], r, sem).wait()
        return c

    lax.fori_loop(0, tm, issue, 0)
    lax.fori_loop(0, tm, drain, 0)
    rt = rt_ref[...]
    y = rt[:, ROUTE_W:ROUTE_W + 1] * buf_ref[0] + rt[:, ROUTE_W + 1:ROUTE_W + 2] * buf_ref[1]
    o_ref[...] = _layer_norm(DEEPNORM_ALPHA * x_ref[...] + y, g_ref[...], b_ref[...])


def _combine(pos, x, route, ys, g, b, tm=512):
    t, d = x.shape
    tm = min(tm, t)
    return pl.pallas_call(
        functools.partial(_combine_kernel, tm=tm),
        grid_spec=pltpu.PrefetchScalarGridSpec(
            num_scalar_prefetch=1,
            grid=(t // tm,),
            in_specs=[pl.BlockSpec((tm, d), lambda i, pos: (i, 0)),
                      pl.BlockSpec((tm, LANES), lambda i, pos: (i, 0)),
                      pl.BlockSpec(memory_space=pl.ANY),
                      pl.BlockSpec((1, d), lambda i, pos: (0, 0)),
                      pl.BlockSpec((1, d), lambda i, pos: (0, 0))],
            out_specs=pl.BlockSpec((tm, d), lambda i, pos: (i, 0)),
            scratch_shapes=[pltpu.VMEM((MOE_TOPK, tm, d), F32), pltpu.SemaphoreType.DMA(())]),
        out_shape=jax.ShapeDtypeStruct((t, d), F32),
        compiler_params=_cparams(("arbitrary",)),
        name="moe_combine",
    )(pos, x, route, ys, g, b)


def _moe(x, route, wg, wu, wd, g, b, tf=512):
    t, d = x.shape
    n_assign = MOE_TOPK * t
    n_rows = n_assign + N_EXPERTS * MOE_TM
    n_tiles = n_rows // MOE_TM
    expert = route[:, ROUTE_IDX:ROUTE_IDX + MOE_TOPK].astype(jnp.int32).reshape(n_assign)
    onehot = (expert[:, None] == jnp.arange(N_EXPERTS, dtype=jnp.int32)[None, :]).astype(jnp.int32)
    rank = jnp.sum((jnp.cumsum(onehot, axis=0) - onehot) * onehot, axis=1)
    count = jnp.sum(onehot, axis=0)
    padded = (count + MOE_TM - 1) // MOE_TM * MOE_TM
    group_end = jnp.cumsum(padded)
    group_start = group_end - padded
    pos = (jnp.sum(onehot * group_start[None, :], axis=1) + rank).astype(jnp.int32)
    tile_start = jnp.arange(n_tiles, dtype=jnp.int32) * MOE_TM
    tile_used = (tile_start < group_end[-1]).astype(jnp.int32)
    tile_expert = jnp.sum((tile_start[:, None] >= group_end[None, :]).astype(jnp.int32), axis=1)
    last_expert = jnp.sum((group_end[-1] - 1 >= group_end).astype(jnp.int32))
    tile_expert = jnp.where(tile_used != 0, tile_expert, last_expert).astype(jnp.int32)
    xs = _dispatch(pos, x, n_rows)
    ys = _moe_experts(tile_expert, tile_used, xs, wg, wu, wd, tf)
    return _combine(pos, x, route, ys, g, b)


def _pad_lanes(v):
    v = v.reshape(1, -1).astype(F32)
    return jnp.pad(v, ((0, 0), (0, LANES - v.shape[1])))


def kernel(x, w_in, conv_w, a_log, dt_bias, dn_norm_w, w_up_a, w_up_b, w_o, rel_bias, ln1_g, ln1_b,
           ln2_g, ln2_b, ffn_w_gate, ffn_w_up, ffn_w_down, router_w, router_b, exp_w_gate, exp_w_up,
           exp_w_down):
    batch, seq, d = x.shape
    t = batch * seq
    assert d == D_MODEL and seq % MOBA_BLOCK == 0 and seq // MOBA_BLOCK <= LANES
    c0 = N_QKV_DN
    c1 = c0 + V_DIM
    c2 = c1 + DN_HEADS
    c3 = c2 + DN_HEADS
    c4 = c3 + 3 * MOBA_DIM
    xf = x.reshape(t, d)
    for layer in range(DEPTH):
        w = w_in[layer]
        w_main = jnp.concatenate([w[:, :c1], w[:, c4:], w[:, c3:c4]], axis=1).astype(BF16)
        w_ab = jnp.pad(w[:, c1:c3], ((0, 0), (0, LANES - 2 * DN_HEADS))).astype(BF16)
        proj = _inproj(xf, w_main)
        gb = _gates(xf, w_ab, _pad_lanes(a_log[layer]), _pad_lanes(dt_bias[layer]))
        dct = gb[:, :DN_HEADS].reshape(t // DN_CHUNK, DN_CHUNK, DN_HEADS).transpose(0, 2, 1)
        qkv = _conv_prep(proj, conv_w[layer], batch, seq)
        o_a = _deltanet(qkv, proj, gb, dct, dn_norm_w[layer].reshape(1, DN_DV), batch, seq)
        o_b = _moba(proj, rel_bias, batch, seq)
        xf = _mix(o_a, o_b, proj, xf, w_up_a[layer].astype(BF16), w_up_b[layer].astype(BF16),
                  w_o[layer].astype(BF16), ln1_g[layer].reshape(1, d), ln1_b[layer].reshape(1, d))
        i = layer // 2
        g2 = ln2_g[layer].reshape(1, d)
        b2 = ln2_b[layer].reshape(1, d)
        if layer % 2 == 0:
            xf = _ffn(xf, ffn_w_gate[i][None].astype(BF16), ffn_w_up[i][None].astype(BF16),
                      ffn_w_down[i][None].astype(BF16), g2, b2, tm=512, tf=1408)
        else:
            rw = jnp.pad(router_w[i], ((0, 0), (0, LANES - N_EXPERTS)))
            route = _router(xf, rw, _pad_lanes(router_b[i]))
            xf = _moe(xf, route, exp_w_gate[i].astype(BF16), exp_w_up[i].astype(BF16),
                      exp_w_down[i].astype(BF16), g2, b2)
    return xf.reshape(batch, seq, d)
```

```python
import functools
import math

import numpy as np
import jax
import jax.numpy as jnp
from jax import lax
from jax.experimental import pallas as pl
from jax.experimental.pallas import tpu as pltpu

F32 = jnp.float32
BF16 = jnp.bfloat16
HIGHEST = lax.Precision.HIGHEST

D_MODEL = 1024
DEPTH = 4
DN_HEADS = 8
DN_DK = 128
DN_DV = 128
DN_CHUNK = 64
CONV_WIDTH = 4
QK_DIM = DN_HEADS * DN_DK
V_DIM = DN_HEADS * DN_DV
N_QKV_DN = 2 * QK_DIM + V_DIM
MOBA_HEADS = 8
MOBA_HD = 64
MOBA_DIM = MOBA_HEADS * MOBA_HD
MOBA_BLOCK = 256
MOBA_TOPK = 3
MOBA_Q_CHUNK = 128
N_BUCKETS = 32
REL_MAX_DIST = 128
N_EXPERTS = 8
DEEPNORM_ALPHA = (2 * DEPTH) ** 0.25
LN_EPS = 1e-5
RMS_EPS = 1e-6

LANES = 128
VMEM_LIMIT = 48 * 1024 * 1024

C_QKV = 0
C_Z = N_QKV_DN
C_GATE = C_Z + V_DIM
C_MB = C_GATE + 2 * D_MODEL
N_PROJ = C_MB + 3 * MOBA_DIM


def _bucket_starts():
    n = np.arange(0, 4 * REL_MAX_DIST)
    max_exact = N_BUCKETS // 2
    t = np.log(np.maximum(n, 1) / max_exact) / math.log(REL_MAX_DIST / max_exact) * (N_BUCKETS - max_exact)
    large = np.minimum(max_exact + t.astype(np.int64), N_BUCKETS - 1)
    b = np.where(n < max_exact, n, large)
    assert np.all(np.diff(b) >= 0) and b[-1] == N_BUCKETS - 1
    return [int(np.argmax(b >= k)) for k in range(N_BUCKETS)]


BUCKET_STARTS = _bucket_starts()
assert BUCKET_STARTS[-1] <= REL_MAX_DIST


def _cparams(sem):
    return pltpu.CompilerParams(dimension_semantics=sem, vmem_limit_bytes=VMEM_LIMIT)


def _mm(a, b, prec=None):
    return jnp.dot(a, b, preferred_element_type=F32, precision=prec)


def _mm_nt(a, b, prec=None):
    return lax.dot_general(a, b, (((1,), (1,)), ((), ())), preferred_element_type=F32, precision=prec)


def _mm_tn(a, b, prec=None):
    return lax.dot_general(a, b, (((0,), (0,)), ((), ())), preferred_element_type=F32, precision=prec)


def _bmm(a, b, prec=None):
    return lax.dot_general(a, b, (((2,), (1,)), ((0,), (0,))), preferred_element_type=F32, precision=prec)


def _bmm_nt(a, b, prec=None):
    return lax.dot_general(a, b, (((2,), (2,)), ((0,), (0,))), preferred_element_type=F32, precision=prec)


def _split(a):
    hi = a.astype(BF16)
    return hi, (a - hi.astype(F32)).astype(BF16)


def _bmm_split(a, b):
    return _bmm(a[0], b[0]) + (_bmm(a[0], b[1]) + _bmm(a[1], b[0]))


def _layer_norm(y, g, b):
    mu = jnp.mean(y, axis=-1, keepdims=True)
    yc = y - mu
    var = jnp.mean(yc * yc, axis=-1, keepdims=True)
    return yc * lax.rsqrt(var + LN_EPS) * g + b


PROJ_TN = 768
GATE_ROWS = 256
CARRY_ROWS = 8


def _proj_kernel(x_ref, w_ref, cw_ref, wab_ref, alog_ref, dtb_ref, o_ref, gb_ref, x16_ref, carry_ref,
                 *, tiles_per_seq):
    i = pl.program_id(0)
    j = pl.program_id(1)
    tm = x_ref.shape[0]
    n_conv = N_QKV_DN // PROJ_TN
    heads_per_tile = PROJ_TN // DN_DK

    @pl.when(j == 0)
    def _():
        x16 = x_ref[...].astype(BF16)
        x16_ref[...] = x16
        p = _mm(x16, wab_ref[...])
        z = p + dtb_ref[...]
        g = -jnp.exp(alog_ref[...]) * (jnp.maximum(z, 0.0) + jnp.log1p(jnp.exp(-jnp.abs(z))))
        r = lax.broadcasted_iota(jnp.int32, (GATE_ROWS, GATE_ROWS), 0)
        c = lax.broadcasted_iota(jnp.int32, (GATE_ROWS, GATE_ROWS), 1)
        tri = ((r // DN_CHUNK == c // DN_CHUNK) & (r >= c)).astype(F32)
        lane = lax.broadcasted_iota(jnp.int32, (GATE_ROWS, LANES), 1)
        for s in range(tm // GATE_ROWS):
            rows = slice(s * GATE_ROWS, (s + 1) * GATE_ROWS)
            dc = _mm(tri, g[rows], HIGHEST)
            gb_ref[rows, :] = jnp.where(lane < DN_HEADS, dc, jax.nn.sigmoid(p[rows]))

    y = _mm(x16_ref[...], w_ref[...])

    @pl.when(j >= n_conv)
    def _():
        o_ref[...] = y

    @pl.when(j < n_conv)
    def _():
        w = cw_ref[...]
        @pl.when(i % tiles_per_seq == 0)
        def _():
            carry_ref[j] = jnp.zeros((CARRY_ROWS, PROJ_TN), F32)

        prev = carry_ref[j]
        carry_ref[j] = y[tm - CARRY_ROWS:, :]
        row = lax.broadcasted_iota(jnp.int32, (CARRY_ROWS, PROJ_TN), 0)
        acc = y * w[CONV_WIDTH - 1:CONV_WIDTH, :]
        for k in range(1, CONV_WIDTH):
            rolled = pltpu.roll(y, k, axis=0)
            top = jnp.where(row < k, pltpu.roll(prev, k, axis=0), rolled[:CARRY_ROWS])
            shifted = jnp.concatenate([top, rolled[CARRY_ROWS:]], axis=0)
            acc = acc + shifted * w[CONV_WIDTH - 1 - k:CONV_WIDTH - k, :]
        acc = acc * jax.nn.sigmoid(acc)
        for h in range(heads_per_tile):
            cols = slice(h * DN_DK, (h + 1) * DN_DK)
            a = acc[:, cols]
            head = j * heads_per_tile + h
            an = a * lax.rsqrt(jnp.sum(a * a, axis=-1, keepdims=True) + RMS_EPS)
            an = an * jnp.where(head < DN_HEADS, DN_DK ** -0.5, 1.0)
            o_ref[:, cols] = jnp.where(head < 2 * DN_HEADS, an, a)


def _proj(x, w, conv_w, w_ab, alog, dtb, seq, tm=1024):
    t, d = x.shape
    n = w.shape[1]
    tm = min(tm, seq)
    n_conv = N_QKV_DN // PROJ_TN
    const = lambda a: pl.BlockSpec(a.shape, lambda i, j: (0, 0))
    return pl.pallas_call(
        functools.partial(_proj_kernel, tiles_per_seq=seq // tm),
        grid=(t // tm, n // PROJ_TN),
        in_specs=[pl.BlockSpec((tm, d), lambda i, j: (i, 0)),
                  pl.BlockSpec((d, PROJ_TN), lambda i, j: (0, j)),
                  pl.BlockSpec((CONV_WIDTH, PROJ_TN), lambda i, j: (0, jnp.minimum(j, n_conv - 1))),
                  const(w_ab), const(alog), const(dtb)],
        out_specs=[pl.BlockSpec((tm, PROJ_TN), lambda i, j: (i, j)),
                   pl.BlockSpec((tm, LANES), lambda i, j: (i, 0))],
        out_shape=[jax.ShapeDtypeStruct((t, n), F32), jax.ShapeDtypeStruct((t, LANES), F32)],
        scratch_shapes=[pltpu.VMEM((tm, d), BF16), pltpu.VMEM((n_conv, CARRY_ROWS, PROJ_TN), F32)],
        compiler_params=_cparams(("arbitrary", "arbitrary")),
        name="proj",
    )(x, w, conv_w, w_ab, alog, dtb)


def _dn_kernel(q_ref, k_ref, v_ref, z_ref, gb_ref, dct_ref, nw_ref, o_ref, s_ref, *, cb):
    @pl.when(pl.program_id(1) == 0)
    def _():
        s_ref[...] = jnp.zeros_like(s_ref)

    r = lax.broadcasted_iota(jnp.int32, (DN_CHUNK, DN_CHUNK), 0)
    c = lax.broadcasted_iota(jnp.int32, (DN_CHUNK, DN_CHUNK), 1)
    lower = r >= c
    strict = r > c
    eye = (r == c).astype(F32)
    nw = nw_ref[...]

    heads = range(DN_HEADS)
    head_cols = [slice(h * DN_DK, (h + 1) * DN_DK) for h in heads]

    chunk_rows = [slice(ci * DN_CHUNK, (ci + 1) * DN_CHUNK) for ci in range(cb)]
    units = [(ci, h) for ci in range(cb) for h in heads]
    gb = gb_ref[...]
    q = jnp.stack([q_ref[chunk_rows[ci], head_cols[h]] for ci, h in units])
    k = jnp.stack([k_ref[chunk_rows[ci], head_cols[h]] for ci, h in units])
    v = jnp.stack([v_ref[chunk_rows[ci], head_cols[h]] for ci, h in units])
    dc = jnp.stack([gb[chunk_rows[ci], h:h + 1] for ci, h in units])
    beta = jnp.stack([gb[chunk_rows[ci], DN_HEADS + h:DN_HEADS + h + 1] for ci, h in units])
    dcr = dct_ref[...].reshape(cb * DN_HEADS, 1, DN_CHUNK)
    dc_last = dc[:, DN_CHUNK - 1:DN_CHUNK, :]
    e_col = jnp.exp(dc)
    dm = jnp.exp(jnp.where(lower[None], dc - dcr, -jnp.inf))
    kb = k * beta
    vb = v * beta
    k16 = k.astype(BF16)
    kk = _bmm_nt(kb.astype(BF16), k16)
    nmat = jnp.where(strict[None], kk * dm, 0.0)
    pw = _split(-nmat)
    tinv = eye[None] - nmat
    for _ in range(5):
        pw = _split(_bmm_split(pw, pw))
        tinv = tinv + _bmm_split(_split(tinv), pw)
    sol = _bmm_split(_split(tinv), _split(jnp.concatenate([vb, kb * e_col], axis=-1)))
    value = sol[:, :, :DN_DV]
    att16 = (_bmm_nt(q.astype(BF16), k16) * dm).astype(BF16)
    kq16 = jnp.concatenate([sol[:, :, DN_DV:], q * e_col], axis=1).astype(BF16)
    kd16 = (k * jnp.exp(dc_last - dc)).astype(BF16)
    decay = jnp.exp(dc_last)

    s = s_ref[...]
    for ci in range(cb):
        us = slice(ci * DN_HEADS, (ci + 1) * DN_HEADS)
        rs = _bmm(kq16[us], s.astype(BF16))
        vn16 = (value[us] - rs[:, :DN_CHUNK]).astype(BF16)
        o = rs[:, DN_CHUNK:] + _bmm(att16[us], vn16)
        upd = jnp.stack([_mm_tn(kd16[ci * DN_HEADS + h], vn16[h]) for h in heads])
        s = s * decay[us] + upd
        o = o * lax.rsqrt(jnp.mean(o * o, axis=-1, keepdims=True) + RMS_EPS) * nw
        for h in heads:
            zz = z_ref[chunk_rows[ci], head_cols[h]]
            o_ref[chunk_rows[ci], head_cols[h]] = (o[h] * (zz * jax.nn.sigmoid(zz))).astype(o_ref.dtype)
    s_ref[...] = s


def _deltanet(proj, gb, dct, nw, batch, seq, cb=4):
    rows = cb * DN_CHUNK
    nblk = seq // rows
    t = batch * seq
    row_map = lambda col: (lambda b, i: (b * nblk + i, col))
    return pl.pallas_call(
        functools.partial(_dn_kernel, cb=cb),
        grid=(batch, nblk),
        in_specs=[pl.BlockSpec((rows, QK_DIM), row_map(0)),
                  pl.BlockSpec((rows, QK_DIM), row_map(1)),
                  pl.BlockSpec((rows, V_DIM), row_map(2)),
                  pl.BlockSpec((rows, V_DIM), row_map(C_Z // V_DIM)),
                  pl.BlockSpec((rows, LANES), row_map(0)),
                  pl.BlockSpec((cb, DN_HEADS, DN_CHUNK), lambda b, i: (b * nblk + i, 0, 0)),
                  pl.BlockSpec((1, DN_DV), lambda b, i: (0, 0))],
        out_specs=pl.BlockSpec((rows, V_DIM), row_map(0)),
        out_shape=jax.ShapeDtypeStruct((t, V_DIM), BF16),
        scratch_shapes=[pltpu.VMEM((DN_HEADS, DN_DK, DN_DV), F32)],
        compiler_params=_cparams(("parallel", "arbitrary")),
        name="deltanet",
    )(proj, proj, proj, proj, gb, dct, nw)


def _rel_bias_tile(bias_ref, head, dist):
    val = jnp.full(dist.shape, bias_ref[head, 0], F32)
    for b in range(1, N_BUCKETS):
        val = jnp.where(dist >= BUCKET_STARTS[b], bias_ref[head, b], val)
    return val


MOBA_NEG = -1e30
MOBA_PAIR = 2 * MOBA_BLOCK


def _moba_kernel(bias_ref, q_ref, k_ref, v_ref, o_ref, ka_ref, v16_ref, bt_ref, s_ref, *, seq):
    hp = pl.program_id(1)
    nb = seq // MOBA_BLOCK
    k = k_ref[...]
    v16_ref[...] = v_ref[...].astype(BF16)
    kmean = jnp.mean(k.reshape(nb, MOBA_BLOCK, LANES), axis=1)
    lane_k = lax.broadcasted_iota(jnp.int32, k.shape, 1)
    blk_k = lax.broadcasted_iota(jnp.int32, k.shape, 0) // MOBA_BLOCK
    qi = lax.broadcasted_iota(jnp.int32, (MOBA_BLOCK, MOBA_BLOCK), 0)
    kj = lax.broadcasted_iota(jnp.int32, (MOBA_BLOCK, MOBA_BLOCK), 1)
    rel = qi - kj
    for hh in range(2):
        aug = (1 - hh) * MOBA_HD
        in_head_k = (lane_k >= hh * MOBA_HD) & (lane_k < (hh + 1) * MOBA_HD)
        ka_ref[hh] = jnp.where(in_head_k, k, (lane_k - aug == blk_k).astype(F32)).astype(BF16)
        b_far = bias_ref[2 * hp + hh, N_BUCKETS - 1]
        own = _rel_bias_tile(bias_ref, 2 * hp + hh, jnp.maximum(rel, 0)) - b_far
        bt_ref[hh, 0] = jnp.where(rel >= 0, own, -jnp.inf)
        bt_ref[hh, 1] = _rel_bias_tile(bias_ref, 2 * hp + hh, rel + MOBA_BLOCK) - b_far
    lane = lax.broadcasted_iota(jnp.int32, (MOBA_BLOCK, LANES), 1)
    blk_t = lax.broadcasted_iota(jnp.int32, (nb, MOBA_BLOCK), 0)

    def lane_tile_max(s):
        parts = [s[:, i * LANES:(i + 1) * LANES] for i in range(s.shape[1] // LANES)]
        return functools.reduce(jnp.maximum, parts)

    def lane_tile_sum(s):
        parts = [s[:, i * LANES:(i + 1) * LANES] for i in range(s.shape[1] // LANES)]
        return functools.reduce(jnp.add, parts)

    def q_block(qb, carry):
        rows = pl.ds(pl.multiple_of(qb * MOBA_BLOCK, MOBA_BLOCK), MOBA_BLOCK)
        q2 = q_ref[rows, :] * (MOBA_HD ** -0.5)
        own_rows = rows
        near_rows = pl.ds(pl.multiple_of(jnp.maximum(qb - 1, 0) * MOBA_BLOCK, MOBA_BLOCK), MOBA_BLOCK)
        npair = qb // 2
        q_far, mts = [], []
        for hh in range(2):
            aug = (1 - hh) * MOBA_HD
            in_head = (lane >= hh * MOBA_HD) & (lane < (hh + 1) * MOBA_HD)
            qh = jnp.where(in_head, q2, 0.0)
            sc = jnp.where(blk_t < qb, _mm_nt(kmean, qh, HIGHEST), -jnp.inf)
            sel = jnp.zeros(sc.shape, F32)
            for _ in range(MOBA_TOPK):
                m = jnp.max(sc, axis=0, keepdims=True)
                cand = jnp.where((sc == m) & (sc > -jnp.inf), blk_t, nb)
                pick = blk_t == jnp.min(cand, axis=0, keepdims=True)
                sel = jnp.where(pick, 1.0, sel)
                sc = jnp.where(pick, -jnp.inf, sc)
            pen_near = jnp.where(sel > 0.0, 0.0, MOBA_NEG)
            pen_far = jnp.where(blk_t == qb - 1, MOBA_NEG, pen_near)

            def place(pen_t):
                parts = [jnp.zeros((aug, MOBA_BLOCK), F32)] if aug else []
                parts += [pen_t, jnp.zeros((LANES - aug - nb, MOBA_BLOCK), F32)]
                return jnp.concatenate(parts, axis=0).T

            q_own = qh.astype(BF16)
            q_near = jnp.where(in_head, q2, place(pen_near)).astype(BF16)
            q_far.append(jnp.where(in_head, q2, place(pen_far)).astype(BF16))
            s_own = _mm_nt(q_own, ka_ref[hh, own_rows, :]) + bt_ref[hh, 0]
            s_near = _mm_nt(q_near, ka_ref[hh, near_rows, :]) + bt_ref[hh, 1]
            s_ref[hh, 0, :, 0:MOBA_BLOCK] = s_own
            s_ref[hh, 0, :, MOBA_BLOCK:MOBA_PAIR] = s_near
            mts.append(jnp.maximum(lane_tile_max(s_own), lane_tile_max(s_near)))

        def far_scores(p, mt):
            ks = pl.ds(pl.multiple_of(p * MOBA_PAIR, MOBA_PAIR), MOBA_PAIR)
            out = []
            for hh in range(2):
                s = _mm_nt(q_far[hh], ka_ref[hh, ks, :])
                s_ref[hh, 1 + p] = s
                out.append(jnp.maximum(mt[hh], lane_tile_max(s)))
            return tuple(out)

        mts = lax.fori_loop(0, npair, far_scores, tuple(mts))
        ms = [jnp.max(mt, axis=-1, keepdims=True) for mt in mts]

        lts, accs = [], []
        for hh in range(2):
            p = jnp.exp(s_ref[hh, 0] - ms[hh])
            p16 = p.astype(BF16)
            lts.append(lane_tile_sum(p))
            accs.append(_mm(p16[:, 0:MOBA_BLOCK], v16_ref[own_rows, :])
                        + _mm(p16[:, MOBA_BLOCK:MOBA_PAIR], v16_ref[near_rows, :]))

        def far_values(p, st):
            ks = pl.ds(pl.multiple_of(p * MOBA_PAIR, MOBA_PAIR), MOBA_PAIR)
            lt, acc = st
            v_pair = v16_ref[ks, :]
            lt_new, acc_new = [], []
            for hh in range(2):
                pp = jnp.exp(s_ref[hh, 1 + p] - ms[hh])
                lt_new.append(lt[hh] + lane_tile_sum(pp))
                acc_new.append(acc[hh] + _mm(pp.astype(BF16), v_pair))
            return tuple(lt_new), tuple(acc_new)

        lts, accs = lax.fori_loop(0, npair, far_values, (tuple(lts), tuple(accs)))
        outs = [accs[hh] / jnp.sum(lts[hh], axis=-1, keepdims=True) for hh in range(2)]
        o_ref[rows, :] = jnp.where(lane < MOBA_HD, outs[0], outs[1]).astype(o_ref.dtype)
        return carry

    lax.fori_loop(0, nb, q_block, 0)


def _moba(proj, rel_bias, batch, seq):
    npair = MOBA_HEADS // 2
    qb, kb, vb = ((C_MB + i * MOBA_DIM) // LANES for i in range(3))
    blk = lambda base: pl.BlockSpec((seq, LANES), lambda b, h: (b, base + h))
    return pl.pallas_call(
        functools.partial(_moba_kernel, seq=seq),
        grid=(batch, npair),
        in_specs=[pl.BlockSpec(memory_space=pltpu.SMEM), blk(qb), blk(kb), blk(vb)],
        out_specs=pl.BlockSpec((seq, LANES), lambda b, h: (b, h)),
        out_shape=jax.ShapeDtypeStruct((batch * seq, MOBA_DIM), BF16),
        scratch_shapes=[pltpu.VMEM((2, seq, LANES), BF16), pltpu.VMEM((seq, LANES), BF16),
                        pltpu.VMEM((2, 2, MOBA_BLOCK, MOBA_BLOCK), F32),
                        pltpu.VMEM((2, 1 + seq // MOBA_PAIR, MOBA_BLOCK, MOBA_PAIR), F32)],
        compiler_params=_cparams(("parallel", "parallel")),
        name="moba",
    )(rel_bias, proj, proj, proj)


def _mix_kernel(oa_ref, ob_ref, ga_ref, gb_ref, x_ref, wa_ref, wb_ref, wo_ref, g_ref, b_ref, o_ref):
    ya = _mm(oa_ref[...], wa_ref[...])
    yb = _mm(ob_ref[...], wb_ref[...])
    m = jax.nn.sigmoid(ga_ref[...]) * ya + jax.nn.sigmoid(gb_ref[...]) * yb
    mix = _mm(m.astype(BF16), wo_ref[...])
    o_ref[...] = _layer_norm(DEEPNORM_ALPHA * x_ref[...] + mix, g_ref[...], b_ref[...])


def _mix(oa, ob, proj, x, wa, wb, wo, g, b, tm=512):
    t, d = x.shape
    row = lambda col: (lambda i: (i, col))
    full = lambda a: pl.BlockSpec(a.shape, lambda i: (0, 0))
    return pl.pallas_call(
        _mix_kernel,
        grid=(t // tm,),
        in_specs=[pl.BlockSpec((tm, V_DIM), row(0)), pl.BlockSpec((tm, MOBA_DIM), row(0)),
                  pl.BlockSpec((tm, d), row(C_GATE // d)), pl.BlockSpec((tm, d), row(C_GATE // d + 1)),
                  pl.BlockSpec((tm, d), row(0)), full(wa), full(wb), full(wo), full(g), full(b)],
        out_specs=pl.BlockSpec((tm, d), row(0)),
        out_shape=jax.ShapeDtypeStruct((t, d), F32),
        compiler_params=_cparams(("parallel",)),
        name="mix",
    )(oa, ob, proj, proj, x, wa, wb, wo, g, b)


ROUTE_IDX = 0
ROUTE_W = 2
MOE_TOPK = 2
MOE_TM = 512


def _router_kernel(x_ref, w_ref, b_ref, o_ref):
    logits = _mm(x_ref[...], w_ref[...], HIGHEST) + b_ref[...]
    lane = lax.broadcasted_iota(jnp.int32, logits.shape, 1)
    sc = jnp.where(lane < N_EXPERTS, logits, -jnp.inf)
    m1 = jnp.max(sc, axis=-1, keepdims=True)
    i1 = jnp.min(jnp.where(sc == m1, lane, LANES), axis=-1, keepdims=True)
    sc2 = jnp.where(lane == i1, -jnp.inf, sc)
    m2 = jnp.max(sc2, axis=-1, keepdims=True)
    i2 = jnp.min(jnp.where(sc2 == m2, lane, LANES), axis=-1, keepdims=True)
    e2 = jnp.exp(m2 - m1)
    den = 1.0 + e2
    out = jnp.where(lane == ROUTE_IDX, i1.astype(F32), jnp.where(lane == ROUTE_IDX + 1, i2.astype(F32), 0.0))
    o_ref[...] = jnp.where(lane == ROUTE_W, 1.0 / den, jnp.where(lane == ROUTE_W + 1, e2 / den, out))


def _router(x, w, b, tm=512):
    t, d = x.shape
    return pl.pallas_call(
        _router_kernel,
        grid=(t // tm,),
        in_specs=[pl.BlockSpec((tm, d), lambda i: (i, 0)),
                  pl.BlockSpec((d, LANES), lambda i: (0, 0)),
                  pl.BlockSpec((1, LANES), lambda i: (0, 0))],
        out_specs=pl.BlockSpec((tm, LANES), lambda i: (i, 0)),
        out_shape=jax.ShapeDtypeStruct((t, LANES), F32),
        compiler_params=_cparams(("parallel",)),
        name="router",
    )(x, w, b)


FF_CHUNK = 256


def _swiglu(x16, wg_ref, wu_ref, wd_ref):
    acc = None
    for c in range(wg_ref.shape[-1] // FF_CHUNK):
        cs = slice(c * FF_CHUNK, (c + 1) * FF_CHUNK)
        hg = _mm(x16, wg_ref[0, :, cs])
        hu = _mm(x16, wu_ref[0, :, cs])
        y = _mm((hg * jax.nn.sigmoid(hg) * hu).astype(BF16), wd_ref[0, cs, :])
        acc = y if acc is None else acc + y
    return acc


def _ffn_kernel(x_ref, wg_ref, wu_ref, wd_ref, g_ref, b_ref, o_ref):
    x = x_ref[...]
    y = _swiglu(x.astype(BF16), wg_ref, wu_ref, wd_ref)
    o_ref[...] = _layer_norm(DEEPNORM_ALPHA * x + y, g_ref[...], b_ref[...])


def _ffn(x, wg, wu, wd, g, b, tm=512):
    t, d = x.shape
    whole = lambda a: pl.BlockSpec(a.shape, lambda i: (0,) * a.ndim, pipeline_mode=pl.Buffered(1))
    return pl.pallas_call(
        _ffn_kernel,
        grid=(t // tm,),
        in_specs=[pl.BlockSpec((tm, d), lambda i: (i, 0)), whole(wg), whole(wu), whole(wd), whole(g), whole(b)],
        out_specs=pl.BlockSpec((tm, d), lambda i: (i, 0)),
        out_shape=jax.ShapeDtypeStruct((t, d), F32),
        compiler_params=_cparams(("parallel",)),
        name="ffn",
    )(x, wg, wu, wd, g, b)


def _row_copy(src_hbm, src_row, dst_ref, dst_row, sem):
    return pltpu.make_async_copy(src_hbm.at[pl.ds(src_row, 1), :], dst_ref.at[pl.ds(dst_row, 1), :], sem)


def _dispatch_kernel(pos_ref, x_ref, init_hbm, xs_hbm, sem, *, rows):
    del init_hbm
    base = pl.program_id(0) * rows

    def issue(r, c):
        for k in range(MOE_TOPK):
            _row_copy(x_ref, r, xs_hbm, pos_ref[MOE_TOPK * (base + r) + k], sem).start(priority=k)
        return c

    def drain(r, c):
        for k in range(MOE_TOPK):
            _row_copy(x_ref, 0, xs_hbm, 0, sem).wait()
        return c

    lax.fori_loop(0, rows, issue, 0)
    lax.fori_loop(0, rows, drain, 0)


def _dispatch(pos, x, n_rows, rows=512):
    t, d = x.shape
    rows = min(rows, t)
    return pl.pallas_call(
        functools.partial(_dispatch_kernel, rows=rows),
        grid_spec=pltpu.PrefetchScalarGridSpec(
            num_scalar_prefetch=1,
            grid=(t // rows,),
            in_specs=[pl.BlockSpec((rows, d), lambda i, pos: (i, 0)), pl.BlockSpec(memory_space=pl.ANY)],
            out_specs=pl.BlockSpec(memory_space=pl.ANY),
            scratch_shapes=[pltpu.SemaphoreType.DMA(())]),
        out_shape=jax.ShapeDtypeStruct((n_rows, d), F32),
        input_output_aliases={2: 0},
        compiler_params=_cparams(("arbitrary",)),
        name="moe_dispatch",
    )(pos, x, jnp.zeros((n_rows, d), F32))


def _moe_kernel(te_ref, tv_ref, x_ref, wg_ref, wu_ref, wd_ref, o_ref, acc_ref, x16_ref):
    del te_ref
    i = pl.program_id(0)
    j = pl.program_id(1)
    last = j == pl.num_programs(1) - 1
    used = tv_ref[i] != 0

    @pl.when(jnp.logical_and(used, j == 0))
    def _():
        x16_ref[...] = x_ref[...].astype(BF16)
        acc_ref[...] = jnp.zeros_like(acc_ref)

    @pl.when(used)
    def _():
        acc_ref[...] += _swiglu(x16_ref[...], wg_ref, wu_ref, wd_ref)

    @pl.when(jnp.logical_and(used, last))
    def _():
        o_ref[...] = acc_ref[...]

    @pl.when(jnp.logical_and(jnp.logical_not(used), last))
    def _():
        o_ref[...] = jnp.zeros_like(o_ref)


def _moe_experts(tile_expert, tile_used, xs, wg, wu, wd, tf):
    p, d = xs.shape
    f = wg.shape[-1]
    nj = f // tf
    col = lambda i, j, te, tv: jnp.where(tv[i] != 0, j, nj - 1)
    return pl.pallas_call(
        _moe_kernel,
        grid_spec=pltpu.PrefetchScalarGridSpec(
            num_scalar_prefetch=2,
            grid=(p // MOE_TM, nj),
            in_specs=[pl.BlockSpec((MOE_TM, d), lambda i, j, te, tv: (i, 0)),
                      pl.BlockSpec((1, d, tf), lambda i, j, te, tv: (te[i], 0, col(i, j, te, tv))),
                      pl.BlockSpec((1, d, tf), lambda i, j, te, tv: (te[i], 0, col(i, j, te, tv))),
                      pl.BlockSpec((1, tf, d), lambda i, j, te, tv: (te[i], col(i, j, te, tv), 0))],
            out_specs=pl.BlockSpec((MOE_TM, d), lambda i, j, te, tv: (i, 0)),
            scratch_shapes=[pltpu.VMEM((MOE_TM, d), F32), pltpu.VMEM((MOE_TM, d), BF16)]),
        out_shape=jax.ShapeDtypeStruct((p, d), F32),
        compiler_params=_cparams(("parallel", "arbitrary")),
        name="moe_experts",
    )(tile_expert, tile_used, xs, wg, wu, wd)


def _combine_kernel(pos_ref, x_ref, rt_ref, ys_hbm, g_ref, b_ref, o_ref, buf_ref, sem, *, tm):
    base = pl.program_id(0) * tm

    def issue(r, c):
        for k in range(MOE_TOPK):
            _row_copy(ys_hbm, pos_ref[MOE_TOPK * (base + r) + k], buf_ref.at[k], r, sem).start(priority=k)
        return c

    def drain(r, c):
        for k in range(MOE_TOPK):
            _row_copy(ys_hbm, 0, buf_ref.at[k], r, sem).wait()
        return c

    lax.fori_loop(0, tm, issue, 0)
    lax.fori_loop(0, tm, drain, 0)
    rt = rt_ref[...]
    y = rt[:, ROUTE_W:ROUTE_W + 1] * buf_ref[0] + rt[:, ROUTE_W + 1:ROUTE_W + 2] * buf_ref[1]
    o_ref[...] = _layer_norm(DEEPNORM_ALPHA * x_ref[...] + y, g_ref[...], b_ref[...])


def _combine(pos, x, route, ys, g, b, tm=512):
    t, d = x.shape
    tm = min(tm, t)
    return pl.pallas_call(
        functools.partial(_combine_kernel, tm=tm),
        grid_spec=pltpu.PrefetchScalarGridSpec(
            num_scalar_prefetch=1,
            grid=(t // tm,),
            in_specs=[pl.BlockSpec((tm, d), lambda i, pos: (i, 0)),
                      pl.BlockSpec((tm, LANES), lambda i, pos: (i, 0)),
                      pl.BlockSpec(memory_space=pl.ANY),
                      pl.BlockSpec((1, d), lambda i, pos: (0, 0)),
                      pl.BlockSpec((1, d), lambda i, pos: (0, 0))],
            out_specs=pl.BlockSpec((tm, d), lambda i, pos: (i, 0)),
            scratch_shapes=[pltpu.VMEM((MOE_TOPK, tm, d), F32), pltpu.SemaphoreType.DMA(())]),
        out_shape=jax.ShapeDtypeStruct((t, d), F32),
        compiler_params=_cparams(("arbitrary",)),
        name="moe_combine",
    )(pos, x, route, ys, g, b)


def _moe(x, route, wg, wu, wd, first_expert, g, b, tf=1792):
    t, d = x.shape
    n_assign = MOE_TOPK * t
    n_rows = n_assign + N_EXPERTS * MOE_TM
    n_tiles = n_rows // MOE_TM
    expert = route[:, ROUTE_IDX:ROUTE_IDX + MOE_TOPK].astype(jnp.int32).reshape(n_assign)
    onehot = (expert[:, None] == jnp.arange(N_EXPERTS, dtype=jnp.int32)[None, :]).astype(jnp.int32)
    rank = jnp.sum((jnp.cumsum(onehot, axis=0) - onehot) * onehot, axis=1)
    count = jnp.sum(onehot, axis=0)
    padded = (count + MOE_TM - 1) // MOE_TM * MOE_TM
    group_end = jnp.cumsum(padded)
    group_start = group_end - padded
    pos = (jnp.sum(onehot * group_start[None, :], axis=1) + rank).astype(jnp.int32)
    tile_start = jnp.arange(n_tiles, dtype=jnp.int32) * MOE_TM
    tile_used = (tile_start < group_end[-1]).astype(jnp.int32)
    tile_expert = jnp.sum((tile_start[:, None] >= group_end[None, :]).astype(jnp.int32), axis=1)
    last_expert = jnp.sum((group_end[-1] - 1 >= group_end).astype(jnp.int32))
    tile_expert = (jnp.where(tile_used != 0, tile_expert, last_expert) + first_expert).astype(jnp.int32)
    xs = _dispatch(pos, x, n_rows)
    ys = _moe_experts(tile_expert, tile_used, xs, wg, wu, wd, tf)
    return _combine(pos, x, route, ys, g, b)


def _pad_lanes(v):
    v = v.reshape(1, -1).astype(F32)
    return jnp.pad(v, ((0, 0), (0, LANES - v.shape[1])))


def kernel(x, w_in, conv_w, a_log, dt_bias, dn_norm_w, w_up_a, w_up_b, w_o, rel_bias, ln1_g, ln1_b,
           ln2_g, ln2_b, ffn_w_gate, ffn_w_up, ffn_w_down, router_w, router_b, exp_w_gate, exp_w_up,
           exp_w_down):
    batch, seq, d = x.shape
    t = batch * seq
    assert d == D_MODEL and seq % MOBA_BLOCK == 0 and seq // MOBA_BLOCK <= LANES
    c0 = N_QKV_DN
    c1 = c0 + V_DIM
    c2 = c1 + DN_HEADS
    c3 = c2 + DN_HEADS
    c4 = c3 + 3 * MOBA_DIM
    xf = x.reshape(t, d)
    ewg, ewu, ewd = (a.astype(BF16).reshape((-1,) + a.shape[2:]) for a in (exp_w_gate, exp_w_up, exp_w_down))
    for layer in range(DEPTH):
        w = w_in[layer]
        w_main = jnp.concatenate([w[:, :c1], w[:, c4:], w[:, c3:c4]], axis=1).astype(BF16)
        w_ab = jnp.pad(w[:, c1:c3], ((0, 0), (0, LANES - 2 * DN_HEADS))).astype(BF16)
        proj, gb = _proj(xf, w_main, conv_w[layer], w_ab, _pad_lanes(a_log[layer]), _pad_lanes(dt_bias[layer]), seq)
        dct = gb[:, :DN_HEADS].reshape(t // DN_CHUNK, DN_CHUNK, DN_HEADS).transpose(0, 2, 1)
        o_a = _deltanet(proj, gb, dct, dn_norm_w[layer].reshape(1, DN_DV), batch, seq)
        o_b = _moba(proj, rel_bias, batch, seq)
        xf = _mix(o_a, o_b, proj, xf, w_up_a[layer].astype(BF16), w_up_b[layer].astype(BF16),
                  w_o[layer].astype(BF16), ln1_g[layer].reshape(1, d), ln1_b[layer].reshape(1, d))
        i = layer // 2
        g2 = ln2_g[layer].reshape(1, d)
        b2 = ln2_b[layer].reshape(1, d)
        if layer % 2 == 0:
            xf = _ffn(xf, ffn_w_gate[i][None].astype(BF16), ffn_w_up[i][None].astype(BF16),
                      ffn_w_down[i][None].astype(BF16), g2, b2)
        else:
            rw = jnp.pad(router_w[i], ((0, 0), (0, LANES - N_EXPERTS)))
            route = _router(xf, rw, _pad_lanes(router_b[i]))
            xf = _moe(xf, route, ewg, ewu, ewd, i * N_EXPERTS, g2, b2)
    return xf.reshape(batch, seq, d)
```

```python
import functools
import math

import numpy as np
import jax
import jax.numpy as jnp
from jax import lax
from jax.experimental import pallas as pl
from jax.experimental.pallas import tpu as pltpu

F32 = jnp.float32
BF16 = jnp.bfloat16
HIGHEST = lax.Precision.HIGHEST

D_MODEL = 1024
DEPTH = 4
DN_HEADS = 8
DN_DK = 128
DN_DV = 128
DN_CHUNK = 64
CONV_WIDTH = 4
QK_DIM = DN_HEADS * DN_DK
V_DIM = DN_HEADS * DN_DV
N_QKV_DN = 2 * QK_DIM + V_DIM
MOBA_HEADS = 8
MOBA_HD = 64
MOBA_DIM = MOBA_HEADS * MOBA_HD
MOBA_BLOCK = 256
MOBA_TOPK = 3
MOBA_Q_CHUNK = 128
N_BUCKETS = 32
REL_MAX_DIST = 128
N_EXPERTS = 8
DEEPNORM_ALPHA = (2 * DEPTH) ** 0.25
LN_EPS = 1e-5
RMS_EPS = 1e-6

LANES = 128
VMEM_LIMIT = 48 * 1024 * 1024

C_QKV = 0
C_Z = N_QKV_DN
C_GATE = C_Z + V_DIM
C_MB = C_GATE + 2 * D_MODEL
N_PROJ = C_MB + 3 * MOBA_DIM


def _bucket_starts():
    n = np.arange(0, 4 * REL_MAX_DIST)
    max_exact = N_BUCKETS // 2
    t = np.log(np.maximum(n, 1) / max_exact) / math.log(REL_MAX_DIST / max_exact) * (N_BUCKETS - max_exact)
    large = np.minimum(max_exact + t.astype(np.int64), N_BUCKETS - 1)
    b = np.where(n < max_exact, n, large)
    assert np.all(np.diff(b) >= 0) and b[-1] == N_BUCKETS - 1
    return [int(np.argmax(b >= k)) for k in range(N_BUCKETS)]


BUCKET_STARTS = _bucket_starts()
assert BUCKET_STARTS[-1] <= REL_MAX_DIST


def _cparams(sem):
    return pltpu.CompilerParams(dimension_semantics=sem, vmem_limit_bytes=VMEM_LIMIT)


def _mm(a, b, prec=None):
    return jnp.dot(a, b, preferred_element_type=F32, precision=prec)


def _mm_nt(a, b, prec=None):
    return lax.dot_general(a, b, (((1,), (1,)), ((), ())), preferred_element_type=F32, precision=prec)


def _mm_tn(a, b, prec=None):
    return lax.dot_general(a, b, (((0,), (0,)), ((), ())), preferred_element_type=F32, precision=prec)


def _bmm(a, b, prec=None):
    return lax.dot_general(a, b, (((2,), (1,)), ((0,), (0,))), preferred_element_type=F32, precision=prec)


def _bmm_nt(a, b, prec=None):
    return lax.dot_general(a, b, (((2,), (2,)), ((0,), (0,))), preferred_element_type=F32, precision=prec)


def _split(a):
    hi = a.astype(BF16)
    return hi, (a - hi.astype(F32)).astype(BF16)


def _bmm_split(a, b):
    return _bmm(a[0], b[0]) + (_bmm(a[0], b[1]) + _bmm(a[1], b[0]))


def _layer_norm(y, g, b):
    mu = jnp.mean(y, axis=-1, keepdims=True)
    yc = y - mu
    var = jnp.mean(yc * yc, axis=-1, keepdims=True)
    return yc * lax.rsqrt(var + LN_EPS) * g + b


PROJ_TN = 768
PROJ_CHUNK = 256
GATE_ROWS = 256
CARRY_ROWS = 8


def _proj_kernel(x_ref, w_ref, cw_ref, wab_ref, alog_ref, dtb_ref, o_ref, gb_ref, x16_ref, carry_ref,
                 *, tiles_per_seq):
    i = pl.program_id(0)
    j = pl.program_id(1)
    tm = x_ref.shape[0]
    n_conv = N_QKV_DN // PROJ_TN
    heads_per_tile = PROJ_TN // DN_DK

    @pl.when(j == 0)
    def _():
        x16 = x_ref[...].astype(BF16)
        x16_ref[...] = x16
        p = _mm(x16, wab_ref[...])
        z = p + dtb_ref[...]
        g = -jnp.exp(alog_ref[...]) * (jnp.maximum(z, 0.0) + jnp.log1p(jnp.exp(-jnp.abs(z))))
        in_chunk = lax.broadcasted_iota(jnp.int32, g.shape, 0) % DN_CHUNK
        step = 1
        while step < DN_CHUNK:
            g = g + jnp.where(in_chunk >= step, pltpu.roll(g, step, axis=0), 0.0)
            step *= 2
        lane = lax.broadcasted_iota(jnp.int32, g.shape, 1)
        gb_ref[...] = jnp.where(lane < DN_HEADS, g, jax.nn.sigmoid(p))

    @pl.when(j >= n_conv)
    def _():
        o_ref[...] = _mm(x16_ref[...], w_ref[...])

    @pl.when(j < n_conv)
    def _():
        @pl.when(i % tiles_per_seq == 0)
        def _():
            carry_ref[j] = jnp.zeros((CARRY_ROWS, PROJ_TN), F32)

        row = lax.broadcasted_iota(jnp.int32, (CARRY_ROWS, PROJ_CHUNK), 0)
        for c in range(PROJ_TN // PROJ_CHUNK):
            cc = slice(c * PROJ_CHUNK, (c + 1) * PROJ_CHUNK)
            y = _mm(x16_ref[...], w_ref[:, cc])
            w = cw_ref[:, cc]
            prev = carry_ref[j, :, cc]
            carry_ref[j, :, cc] = y[tm - CARRY_ROWS:, :]
            acc = y * w[CONV_WIDTH - 1:CONV_WIDTH, :]
            for k in range(1, CONV_WIDTH):
                rolled = pltpu.roll(y, k, axis=0)
                top = jnp.where(row < k, pltpu.roll(prev, k, axis=0), rolled[:CARRY_ROWS])
                shifted = jnp.concatenate([top, rolled[CARRY_ROWS:]], axis=0)
                acc = acc + shifted * w[CONV_WIDTH - 1 - k:CONV_WIDTH - k, :]
            acc = acc * jax.nn.sigmoid(acc)
            for h in range(PROJ_CHUNK // DN_DK):
                a = acc[:, h * DN_DK:(h + 1) * DN_DK]
                head = j * heads_per_tile + c * (PROJ_CHUNK // DN_DK) + h
                an = a * lax.rsqrt(jnp.sum(a * a, axis=-1, keepdims=True) + RMS_EPS)
                an = an * jnp.where(head < DN_HEADS, DN_DK ** -0.5, 1.0)
                col = c * PROJ_CHUNK + h * DN_DK
                o_ref[:, col:col + DN_DK] = jnp.where(head < 2 * DN_HEADS, an, a)


def _proj(x, w, conv_w, w_ab, alog, dtb, seq, tm=1024):
    t, d = x.shape
    n = w.shape[1]
    tm = min(tm, seq)
    n_conv = N_QKV_DN // PROJ_TN
    const = lambda a: pl.BlockSpec(a.shape, lambda i, j: (0, 0))
    return pl.pallas_call(
        functools.partial(_proj_kernel, tiles_per_seq=seq // tm),
        grid=(t // tm, n // PROJ_TN),
        in_specs=[pl.BlockSpec((tm, d), lambda i, j: (i, 0)),
                  pl.BlockSpec((d, PROJ_TN), lambda i, j: (0, j)),
                  pl.BlockSpec((CONV_WIDTH, PROJ_TN), lambda i, j: (0, jnp.minimum(j, n_conv - 1))),
                  const(w_ab), const(alog), const(dtb)],
        out_specs=[pl.BlockSpec((tm, PROJ_TN), lambda i, j: (i, j)),
                   pl.BlockSpec((tm, LANES), lambda i, j: (i, 0))],
        out_shape=[jax.ShapeDtypeStruct((t, n), F32), jax.ShapeDtypeStruct((t, LANES), F32)],
        scratch_shapes=[pltpu.VMEM((tm, d), BF16), pltpu.VMEM((n_conv, CARRY_ROWS, PROJ_TN), F32)],
        compiler_params=_cparams(("arbitrary", "arbitrary")),
        name="proj",
    )(x, w, conv_w, w_ab, alog, dtb)


def _dn_kernel(q_ref, k_ref, v_ref, z_ref, gb_ref, dct_ref, nw_ref, o_ref, s_ref, *, cb):
    @pl.when(pl.program_id(1) == 0)
    def _():
        s_ref[...] = jnp.zeros_like(s_ref)

    r = lax.broadcasted_iota(jnp.int32, (DN_CHUNK, DN_CHUNK), 0)
    c = lax.broadcasted_iota(jnp.int32, (DN_CHUNK, DN_CHUNK), 1)
    lower = r >= c
    strict = r > c
    eye = (r == c).astype(F32)
    nw = nw_ref[...]

    heads = range(DN_HEADS)
    head_cols = [slice(h * DN_DK, (h + 1) * DN_DK) for h in heads]

    chunk_rows = [slice(ci * DN_CHUNK, (ci + 1) * DN_CHUNK) for ci in range(cb)]
    units = [(ci, h) for ci in range(cb) for h in heads]
    gb = gb_ref[...]
    q = jnp.stack([q_ref[chunk_rows[ci], head_cols[h]] for ci, h in units])
    k = jnp.stack([k_ref[chunk_rows[ci], head_cols[h]] for ci, h in units])
    v = jnp.stack([v_ref[chunk_rows[ci], head_cols[h]] for ci, h in units])
    dc = jnp.stack([gb[chunk_rows[ci], h:h + 1] for ci, h in units])
    beta = jnp.stack([gb[chunk_rows[ci], DN_HEADS + h:DN_HEADS + h + 1] for ci, h in units])
    dcr = dct_ref[...].reshape(cb * DN_HEADS, 1, DN_CHUNK)
    dc_last = dc[:, DN_CHUNK - 1:DN_CHUNK, :]
    e_col = jnp.exp(dc)
    dm = jnp.exp(jnp.where(lower[None], dc - dcr, -jnp.inf))
    kb = k * beta
    vb = v * beta
    k16 = k.astype(BF16)
    kk = _bmm_nt(kb.astype(BF16), k16)
    nmat = jnp.where(strict[None], kk * dm, 0.0)
    pw = _split(-nmat)
    tinv = eye[None] - nmat
    for _ in range(5):
        pw = _split(_bmm_split(pw, pw))
        tinv = tinv + _bmm_split(_split(tinv), pw)
    sol = _bmm_split(_split(tinv), _split(jnp.concatenate([vb, kb * e_col], axis=-1)))
    value = sol[:, :, :DN_DV]
    att16 = (_bmm_nt(q.astype(BF16), k16) * dm).astype(BF16)
    kq16 = jnp.concatenate([sol[:, :, DN_DV:], q * e_col], axis=1).astype(BF16)
    kd16 = (k * jnp.exp(dc_last - dc)).astype(BF16)
    decay = jnp.exp(dc_last)

    s = s_ref[...]
    for ci in range(cb):
        us = slice(ci * DN_HEADS, (ci + 1) * DN_HEADS)
        rs = _bmm(kq16[us], s.astype(BF16))
        vn16 = (value[us] - rs[:, :DN_CHUNK]).astype(BF16)
        o = rs[:, DN_CHUNK:] + _bmm(att16[us], vn16)
        upd = jnp.stack([_mm_tn(kd16[ci * DN_HEADS + h], vn16[h]) for h in heads])
        s = s * decay[us] + upd
        o = o * lax.rsqrt(jnp.mean(o * o, axis=-1, keepdims=True) + RMS_EPS) * nw
        for h in heads:
            zz = z_ref[chunk_rows[ci], head_cols[h]]
            o_ref[chunk_rows[ci], head_cols[h]] = (o[h] * (zz * jax.nn.sigmoid(zz))).astype(o_ref.dtype)
    s_ref[...] = s


def _deltanet(proj, gb, dct, nw, batch, seq, cb=4):
    rows = cb * DN_CHUNK
    nblk = seq // rows
    t = batch * seq
    row_map = lambda col: (lambda b, i: (b * nblk + i, col))
    return pl.pallas_call(
        functools.partial(_dn_kernel, cb=cb),
        grid=(batch, nblk),
        in_specs=[pl.BlockSpec((rows, QK_DIM), row_map(0)),
                  pl.BlockSpec((rows, QK_DIM), row_map(1)),
                  pl.BlockSpec((rows, V_DIM), row_map(2)),
                  pl.BlockSpec((rows, V_DIM), row_map(C_Z // V_DIM)),
                  pl.BlockSpec((rows, LANES), row_map(0)),
                  pl.BlockSpec((cb, DN_HEADS, DN_CHUNK), lambda b, i: (b * nblk + i, 0, 0)),
                  pl.BlockSpec((1, DN_DV), lambda b, i: (0, 0))],
        out_specs=pl.BlockSpec((rows, V_DIM), row_map(0)),
        out_shape=jax.ShapeDtypeStruct((t, V_DIM), BF16),
        scratch_shapes=[pltpu.VMEM((DN_HEADS, DN_DK, DN_DV), F32)],
        compiler_params=_cparams(("parallel", "arbitrary")),
        name="deltanet",
    )(proj, proj, proj, proj, gb, dct, nw)


def _rel_bias_tile(bias_ref, head, dist):
    val = jnp.full(dist.shape, bias_ref[head, 0], F32)
    for b in range(1, N_BUCKETS):
        val = jnp.where(dist >= BUCKET_STARTS[b], bias_ref[head, b], val)
    return val


MOBA_NEG = -1e30
MOBA_PAIR = 2 * MOBA_BLOCK


def _moba_kernel(bias_ref, q_ref, k_ref, v_ref, o_ref, ka_ref, v16_ref, bt_ref, s_ref, qv_ref, *, seq):
    hp = pl.program_id(1)
    nb = seq // MOBA_BLOCK
    k = k_ref[...]
    v16_ref[...] = v_ref[...].astype(BF16)
    kmean = jnp.mean(k.reshape(nb, MOBA_BLOCK, LANES), axis=1)
    lane_k = lax.broadcasted_iota(jnp.int32, k.shape, 1)
    blk_k = lax.broadcasted_iota(jnp.int32, k.shape, 0) // MOBA_BLOCK
    qi = lax.broadcasted_iota(jnp.int32, (MOBA_BLOCK, MOBA_BLOCK), 0)
    kj = lax.broadcasted_iota(jnp.int32, (MOBA_BLOCK, MOBA_BLOCK), 1)
    rel = qi - kj
    for hh in range(2):
        aug = (1 - hh) * MOBA_HD
        in_head_k = (lane_k >= hh * MOBA_HD) & (lane_k < (hh + 1) * MOBA_HD)
        ka_ref[hh] = jnp.where(in_head_k, k, (lane_k - aug == blk_k).astype(F32)).astype(BF16)
        b_far = bias_ref[2 * hp + hh, N_BUCKETS - 1]
        own = _rel_bias_tile(bias_ref, 2 * hp + hh, jnp.maximum(rel, 0)) - b_far
        bt_ref[hh, 0] = jnp.where(rel >= 0, own, -jnp.inf)
        bt_ref[hh, 1] = _rel_bias_tile(bias_ref, 2 * hp + hh, rel + MOBA_BLOCK) - b_far
    lane = lax.broadcasted_iota(jnp.int32, (MOBA_BLOCK, LANES), 1)

    slab = min(4 * MOBA_BLOCK, seq)
    blk_t = lax.broadcasted_iota(jnp.int32, (nb, slab), 0)
    lane_q = lax.broadcasted_iota(jnp.int32, (slab, LANES), 1)
    for sb in range(seq // slab):
        srows = slice(sb * slab, (sb + 1) * slab)
        q2 = q_ref[srows, :] * (MOBA_HD ** -0.5)
        qblk = (lax.broadcasted_iota(jnp.int32, (nb, slab), 1) + sb * slab) // MOBA_BLOCK
        for hh in range(2):
            aug = (1 - hh) * MOBA_HD
            in_head = (lane_q >= hh * MOBA_HD) & (lane_q < (hh + 1) * MOBA_HD)
            qh = jnp.where(in_head, q2, 0.0)
            sc = jnp.where(blk_t < qblk, _mm_nt(kmean, qh, HIGHEST), -jnp.inf)
            sel = jnp.zeros(sc.shape, F32)
            for _ in range(MOBA_TOPK):
                m = jnp.max(sc, axis=0, keepdims=True)
                cand = jnp.where((sc == m) & (sc > -jnp.inf), blk_t, nb)
                pick = blk_t == jnp.min(cand, axis=0, keepdims=True)
                sel = jnp.where(pick, 1.0, sel)
                sc = jnp.where(pick, -jnp.inf, sc)
            pen_near = jnp.where(sel > 0.0, 0.0, MOBA_NEG)
            pen_far = jnp.where(blk_t == qblk - 1, MOBA_NEG, pen_near)

            def place(pen_t):
                parts = [jnp.zeros((aug, slab), F32)] if aug else []
                parts += [pen_t, jnp.zeros((LANES - aug - nb, slab), F32)]
                return jnp.concatenate(parts, axis=0).T

            qv_ref[hh, 0, srows, :] = qh.astype(BF16)
            qv_ref[hh, 1, srows, :] = jnp.where(in_head, q2, place(pen_near)).astype(BF16)
            qv_ref[hh, 2, srows, :] = jnp.where(in_head, q2, place(pen_far)).astype(BF16)

    def lane_tile_max(s):
        parts = [s[:, i * LANES:(i + 1) * LANES] for i in range(s.shape[1] // LANES)]
        return functools.reduce(jnp.maximum, parts)

    def lane_tile_sum(s):
        parts = [s[:, i * LANES:(i + 1) * LANES] for i in range(s.shape[1] // LANES)]
        return functools.reduce(jnp.add, parts)

    def q_block(qb, carry):
        rows = pl.ds(pl.multiple_of(qb * MOBA_BLOCK, MOBA_BLOCK), MOBA_BLOCK)
        own_rows = rows
        near_rows = pl.ds(pl.multiple_of(jnp.maximum(qb - 1, 0) * MOBA_BLOCK, MOBA_BLOCK), MOBA_BLOCK)
        npair = qb // 2
        q_far, mts = [], []
        for hh in range(2):
            q_far.append(qv_ref[hh, 2, rows, :])
            s_own = _mm_nt(qv_ref[hh, 0, rows, :], ka_ref[hh, own_rows, :]) + bt_ref[hh, 0]
            s_near = _mm_nt(qv_ref[hh, 1, rows, :], ka_ref[hh, near_rows, :]) + bt_ref[hh, 1]
            s_ref[hh, 0, :, 0:MOBA_BLOCK] = s_own
            s_ref[hh, 0, :, MOBA_BLOCK:MOBA_PAIR] = s_near
            mts.append(jnp.maximum(lane_tile_max(s_own), lane_tile_max(s_near)))

        def far_scores(p, mt):
            ks = pl.ds(pl.multiple_of(p * MOBA_PAIR, MOBA_PAIR), MOBA_PAIR)
            out = []
            for hh in range(2):
                s = _mm_nt(q_far[hh], ka_ref[hh, ks, :])
                s_ref[hh, 1 + p] = s
                out.append(jnp.maximum(mt[hh], lane_tile_max(s)))
            return tuple(out)

        mts = lax.fori_loop(0, npair, far_scores, tuple(mts))
        ms = [jnp.max(mt, axis=-1, keepdims=True) for mt in mts]

        lts, accs = [], []
        for hh in range(2):
            p = jnp.exp(s_ref[hh, 0] - ms[hh])
            p16 = p.astype(BF16)
            lts.append(lane_tile_sum(p))
            accs.append(_mm(p16[:, 0:MOBA_BLOCK], v16_ref[own_rows, :])
                        + _mm(p16[:, MOBA_BLOCK:MOBA_PAIR], v16_ref[near_rows, :]))

        def far_values(p, st):
            ks = pl.ds(pl.multiple_of(p * MOBA_PAIR, MOBA_PAIR), MOBA_PAIR)
            lt, acc = st
            v_pair = v16_ref[ks, :]
            lt_new, acc_new = [], []
            for hh in range(2):
                pp = jnp.exp(s_ref[hh, 1 + p] - ms[hh])
                lt_new.append(lt[hh] + lane_tile_sum(pp))
                acc_new.append(acc[hh] + _mm(pp.astype(BF16), v_pair))
            return tuple(lt_new), tuple(acc_new)

        lts, accs = lax.fori_loop(0, npair, far_values, (tuple(lts), tuple(accs)))
        outs = [accs[hh] / jnp.sum(lts[hh], axis=-1, keepdims=True) for hh in range(2)]
        o_ref[rows, :] = jnp.where(lane < MOBA_HD, outs[0], outs[1]).astype(o_ref.dtype)
        return carry

    lax.fori_loop(0, nb, q_block, 0)


def _moba(proj, rel_bias, batch, seq):
    npair = MOBA_HEADS // 2
    qb, kb, vb = ((C_MB + i * MOBA_DIM) // LANES for i in range(3))
    blk = lambda base: pl.BlockSpec((seq, LANES), lambda b, h: (b, base + h))
    return pl.pallas_call(
        functools.partial(_moba_kernel, seq=seq),
        grid=(batch, npair),
        in_specs=[pl.BlockSpec(memory_space=pltpu.SMEM), blk(qb), blk(kb), blk(vb)],
        out_specs=pl.BlockSpec((seq, LANES), lambda b, h: (b, h)),
        out_shape=jax.ShapeDtypeStruct((batch * seq, MOBA_DIM), BF16),
        scratch_shapes=[pltpu.VMEM((2, seq, LANES), BF16), pltpu.VMEM((seq, LANES), BF16),
                        pltpu.VMEM((2, 2, MOBA_BLOCK, MOBA_BLOCK), F32),
                        pltpu.VMEM((2, 1 + seq // MOBA_PAIR, MOBA_BLOCK, MOBA_PAIR), F32),
                        pltpu.VMEM((2, 3, seq, LANES), BF16)],
        compiler_params=_cparams(("parallel", "parallel")),
        name="moba",
    )(rel_bias, proj, proj, proj)


def _mix_kernel(oa_ref, ob_ref, ga_ref, gb_ref, x_ref, wa_ref, wb_ref, wo_ref, g_ref, b_ref, o_ref):
    ya = _mm(oa_ref[...], wa_ref[...])
    yb = _mm(ob_ref[...], wb_ref[...])
    m = jax.nn.sigmoid(ga_ref[...]) * ya + jax.nn.sigmoid(gb_ref[...]) * yb
    mix = _mm(m.astype(BF16), wo_ref[...])
    o_ref[...] = _layer_norm(DEEPNORM_ALPHA * x_ref[...] + mix, g_ref[...], b_ref[...])


def _mix(oa, ob, proj, x, wa, wb, wo, g, b, tm=512):
    t, d = x.shape
    row = lambda col: (lambda i: (i, col))
    full = lambda a: pl.BlockSpec(a.shape, lambda i: (0, 0))
    return pl.pallas_call(
        _mix_kernel,
        grid=(t // tm,),
        in_specs=[pl.BlockSpec((tm, V_DIM), row(0)), pl.BlockSpec((tm, MOBA_DIM), row(0)),
                  pl.BlockSpec((tm, d), row(C_GATE // d)), pl.BlockSpec((tm, d), row(C_GATE // d + 1)),
                  pl.BlockSpec((tm, d), row(0)), full(wa), full(wb), full(wo), full(g), full(b)],
        out_specs=pl.BlockSpec((tm, d), row(0)),
        out_shape=jax.ShapeDtypeStruct((t, d), F32),
        compiler_params=_cparams(("parallel",)),
        name="mix",
    )(oa, ob, proj, proj, x, wa, wb, wo, g, b)


ROUTE_IDX = 0
ROUTE_W = 2
MOE_TOPK = 2
MOE_TM = 512


def _router_kernel(x_ref, w_ref, b_ref, o_ref):
    logits = _mm(x_ref[...], w_ref[...], HIGHEST) + b_ref[...]
    lane = lax.broadcasted_iota(jnp.int32, logits.shape, 1)
    sc = jnp.where(lane < N_EXPERTS, logits, -jnp.inf)
    m1 = jnp.max(sc, axis=-1, keepdims=True)
    i1 = jnp.min(jnp.where(sc == m1, lane, LANES), axis=-1, keepdims=True)
    sc2 = jnp.where(lane == i1, -jnp.inf, sc)
    m2 = jnp.max(sc2, axis=-1, keepdims=True)
    i2 = jnp.min(jnp.where(sc2 == m2, lane, LANES), axis=-1, keepdims=True)
    e2 = jnp.exp(m2 - m1)
    den = 1.0 + e2
    out = jnp.where(lane == ROUTE_IDX, i1.astype(F32), jnp.where(lane == ROUTE_IDX + 1, i2.astype(F32), 0.0))
    o_ref[...] = jnp.where(lane == ROUTE_W, 1.0 / den, jnp.where(lane == ROUTE_W + 1, e2 / den, out))


def _router(x, w, b, tm=512):
    t, d = x.shape
    return pl.pallas_call(
        _router_kernel,
        grid=(t // tm,),
        in_specs=[pl.BlockSpec((tm, d), lambda i: (i, 0)),
                  pl.BlockSpec((d, LANES), lambda i: (0, 0)),
                  pl.BlockSpec((1, LANES), lambda i: (0, 0))],
        out_specs=pl.BlockSpec((tm, LANES), lambda i: (i, 0)),
        out_shape=jax.ShapeDtypeStruct((t, LANES), F32),
        compiler_params=_cparams(("parallel",)),
        name="router",
    )(x, w, b)


FF_CHUNK = 256


def _swiglu(x16, wg_ref, wu_ref, wd_ref):
    acc = None
    for c in range(wg_ref.shape[-1] // FF_CHUNK):
        cs = slice(c * FF_CHUNK, (c + 1) * FF_CHUNK)
        hg = _mm(x16, wg_ref[0, :, cs])
        hu = _mm(x16, wu_ref[0, :, cs])
        y = _mm((hg * jax.nn.sigmoid(hg) * hu).astype(BF16), wd_ref[0, cs, :])
        acc = y if acc is None else acc + y
    return acc


def _ffn_kernel(x_ref, wg_ref, wu_ref, wd_ref, g_ref, b_ref, o_ref):
    x = x_ref[...]
    y = _swiglu(x.astype(BF16), wg_ref, wu_ref, wd_ref)
    o_ref[...] = _layer_norm(DEEPNORM_ALPHA * x + y, g_ref[...], b_ref[...])


def _ffn(x, wg, wu, wd, g, b, tm=512):
    t, d = x.shape
    whole = lambda a: pl.BlockSpec(a.shape, lambda i: (0,) * a.ndim, pipeline_mode=pl.Buffered(1))
    return pl.pallas_call(
        _ffn_kernel,
        grid=(t // tm,),
        in_specs=[pl.BlockSpec((tm, d), lambda i: (i, 0)), whole(wg), whole(wu), whole(wd), whole(g), whole(b)],
        out_specs=pl.BlockSpec((tm, d), lambda i: (i, 0)),
        out_shape=jax.ShapeDtypeStruct((t, d), F32),
        compiler_params=_cparams(("parallel",)),
        name="ffn",
    )(x, wg, wu, wd, g, b)


DMA_UNROLL = 8


def _row_copy(src_hbm, src_row, dst_ref, dst_row, sem):
    return pltpu.make_async_copy(src_hbm.at[pl.ds(src_row, 1), :], dst_ref.at[pl.ds(dst_row, 1), :], sem)


def _dispatch_kernel(pos_ref, x_ref, init_hbm, xs_hbm, sem, *, rows):
    del init_hbm
    base = pl.program_id(0) * rows

    def issue(r, c):
        for k in range(MOE_TOPK):
            _row_copy(x_ref, r, xs_hbm, pos_ref[MOE_TOPK * (base + r) + k], sem).start(priority=k)
        return c

    def drain(r, c):
        for k in range(MOE_TOPK):
            _row_copy(x_ref, 0, xs_hbm, 0, sem).wait()
        return c

    lax.fori_loop(0, rows, issue, 0, unroll=DMA_UNROLL)
    lax.fori_loop(0, rows, drain, 0)


def _dispatch(pos, x, n_rows, rows=512):
    t, d = x.shape
    rows = min(rows, t)
    return pl.pallas_call(
        functools.partial(_dispatch_kernel, rows=rows),
        grid_spec=pltpu.PrefetchScalarGridSpec(
            num_scalar_prefetch=1,
            grid=(t // rows,),
            in_specs=[pl.BlockSpec((rows, d), lambda i, pos: (i, 0)), pl.BlockSpec(memory_space=pl.ANY)],
            out_specs=pl.BlockSpec(memory_space=pl.ANY),
            scratch_shapes=[pltpu.SemaphoreType.DMA(())]),
        out_shape=jax.ShapeDtypeStruct((n_rows, d), F32),
        input_output_aliases={2: 0},
        compiler_params=_cparams(("arbitrary",)),
        name="moe_dispatch",
    )(pos, x, jnp.zeros((n_rows, d), F32))


def _moe_kernel(te_ref, tv_ref, x_ref, wg_ref, wu_ref, wd_ref, o_ref, acc_ref, x16_ref):
    del te_ref
    i = pl.program_id(0)
    j = pl.program_id(1)
    last = j == pl.num_programs(1) - 1
    used = tv_ref[i] != 0

    @pl.when(jnp.logical_and(used, j == 0))
    def _():
        x16_ref[...] = x_ref[...].astype(BF16)
        acc_ref[...] = jnp.zeros_like(acc_ref)

    @pl.when(used)
    def _():
        acc_ref[...] += _swiglu(x16_ref[...], wg_ref, wu_ref, wd_ref)

    @pl.when(jnp.logical_and(used, last))
    def _():
        o_ref[...] = acc_ref[...]

    @pl.when(jnp.logical_and(jnp.logical_not(used), last))
    def _():
        o_ref[...] = jnp.zeros_like(o_ref)


def _moe_experts(tile_expert, tile_used, xs, wg, wu, wd, tf):
    p, d = xs.shape
    f = wg.shape[-1]
    nj = f // tf
    col = lambda i, j, te, tv: jnp.where(tv[i] != 0, j, nj - 1)
    return pl.pallas_call(
        _moe_kernel,
        grid_spec=pltpu.PrefetchScalarGridSpec(
            num_scalar_prefetch=2,
            grid=(p // MOE_TM, nj),
            in_specs=[pl.BlockSpec((MOE_TM, d), lambda i, j, te, tv: (i, 0)),
                      pl.BlockSpec((1, d, tf), lambda i, j, te, tv: (te[i], 0, col(i, j, te, tv))),
                      pl.BlockSpec((1, d, tf), lambda i, j, te, tv: (te[i], 0, col(i, j, te, tv))),
                      pl.BlockSpec((1, tf, d), lambda i, j, te, tv: (te[i], col(i, j, te, tv), 0))],
            out_specs=pl.BlockSpec((MOE_TM, d), lambda i, j, te, tv: (i, 0)),
            scratch_shapes=[pltpu.VMEM((MOE_TM, d), F32), pltpu.VMEM((MOE_TM, d), BF16)]),
        out_shape=jax.ShapeDtypeStruct((p, d), F32),
        compiler_params=_cparams(("parallel", "arbitrary")),
        name="moe_experts",
    )(tile_expert, tile_used, xs, wg, wu, wd)


def _combine_kernel(pos_ref, x_ref, rt_ref, ys_hbm, g_ref, b_ref, o_ref, buf_ref, sem, *, tm):
    base = pl.program_id(0) * tm

    def issue(r, c):
        for k in range(MOE_TOPK):
            _row_copy(ys_hbm, pos_ref[MOE_TOPK * (base + r) + k], buf_ref.at[k], r, sem).start(priority=k)
        return c

    def drain(r, c):
        for k in range(MOE_TOPK):
            _row_copy(ys_hbm, 0, buf_ref.at[k], r, sem).wait()
        return c

    lax.fori_loop(0, tm, issue, 0, unroll=DMA_UNROLL)
    lax.fori_loop(0, tm, drain, 0)
    rt = rt_ref[...]
    y = rt[:, ROUTE_W:ROUTE_W + 1] * buf_ref[0] + rt[:, ROUTE_W + 1:ROUTE_W + 2] * buf_ref[1]
    o_ref[...] = _layer_norm(DEEPNORM_ALPHA * x_ref[...] + y, g_ref[...], b_ref[...])


def _combine(pos, x, route, ys, g, b, tm=512):
    t, d = x.shape
    tm = min(tm, t)
    return pl.pallas_call(
        functools.partial(_combine_kernel, tm=tm),
        grid_spec=pltpu.PrefetchScalarGridSpec(
            num_scalar_prefetch=1,
            grid=(t // tm,),
            in_specs=[pl.BlockSpec((tm, d), lambda i, pos: (i, 0)),
                      pl.BlockSpec((tm, LANES), lambda i, pos: (i, 0)),
                      pl.BlockSpec(memory_space=pl.ANY),
                      pl.BlockSpec((1, d), lambda i, pos: (0, 0)),
                      pl.BlockSpec((1, d), lambda i, pos: (0, 0))],
            out_specs=pl.BlockSpec((tm, d), lambda i, pos: (i, 0)),
            scratch_shapes=[pltpu.VMEM((MOE_TOPK, tm, d), F32), pltpu.SemaphoreType.DMA(())]),
        out_shape=jax.ShapeDtypeStruct((t, d), F32),
        compiler_params=_cparams(("arbitrary",)),
        name="moe_combine",
    )(pos, x, route, ys, g, b)


def _moe(x, route, wg, wu, wd, first_expert, g, b, tf=1792):
    t, d = x.shape
    n_assign = MOE_TOPK * t
    n_rows = n_assign + N_EXPERTS * MOE_TM
    n_tiles = n_rows // MOE_TM
    expert = route[:, ROUTE_IDX:ROUTE_IDX + MOE_TOPK].astype(jnp.int32).reshape(n_assign)
    onehot = (expert[:, None] == jnp.arange(N_EXPERTS, dtype=jnp.int32)[None, :]).astype(jnp.int32)
    rank = jnp.sum((jnp.cumsum(onehot, axis=0) - onehot) * onehot, axis=1)
    count = jnp.sum(onehot, axis=0)
    padded = (count + MOE_TM - 1) // MOE_TM * MOE_TM
    group_end = jnp.cumsum(padded)
    group_start = group_end - padded
    pos = (jnp.sum(onehot * group_start[None, :], axis=1) + rank).astype(jnp.int32)
    tile_start = jnp.arange(n_tiles, dtype=jnp.int32) * MOE_TM
    tile_used = (tile_start < group_end[-1]).astype(jnp.int32)
    tile_expert = jnp.sum((tile_start[:, None] >= group_end[None, :]).astype(jnp.int32), axis=1)
    last_expert = jnp.sum((group_end[-1] - 1 >= group_end).astype(jnp.int32))
    tile_expert = (jnp.where(tile_used != 0, tile_expert, last_expert) + first_expert).astype(jnp.int32)
    xs = _dispatch(pos, x, n_rows)
    ys = _moe_experts(tile_expert, tile_used, xs, wg, wu, wd, tf)
    return _combine(pos, x, route, ys, g, b)


def _pad_lanes(v):
    v = v.reshape(1, -1).astype(F32)
    return jnp.pad(v, ((0, 0), (0, LANES - v.shape[1])))


def kernel(x, w_in, conv_w, a_log, dt_bias, dn_norm_w, w_up_a, w_up_b, w_o, rel_bias, ln1_g, ln1_b,
           ln2_g, ln2_b, ffn_w_gate, ffn_w_up, ffn_w_down, router_w, router_b, exp_w_gate, exp_w_up,
           exp_w_down):
    batch, seq, d = x.shape
    t = batch * seq
    assert d == D_MODEL and seq % MOBA_BLOCK == 0 and seq // MOBA_BLOCK <= LANES
    c0 = N_QKV_DN
    c1 = c0 + V_DIM
    c2 = c1 + DN_HEADS
    c3 = c2 + DN_HEADS
    c4 = c3 + 3 * MOBA_DIM
    xf = x.reshape(t, d)
    ewg, ewu, ewd = (a.astype(BF16).reshape((-1,) + a.shape[2:]) for a in (exp_w_gate, exp_w_up, exp_w_down))
    for layer in range(DEPTH):
        w = w_in[layer]
        w_main = jnp.concatenate([w[:, :c1], w[:, c4:], w[:, c3:c4]], axis=1).astype(BF16)
        w_ab = jnp.pad(w[:, c1:c3], ((0, 0), (0, LANES - 2 * DN_HEADS))).astype(BF16)
        proj, gb = _proj(xf, w_main, conv_w[layer], w_ab, _pad_lanes(a_log[layer]), _pad_lanes(dt_bias[layer]), seq)
        dct = gb[:, :DN_HEADS].reshape(t // DN_CHUNK, DN_CHUNK, DN_HEADS).transpose(0, 2, 1)
        o_a = _deltanet(proj, gb, dct, dn_norm_w[layer].reshape(1, DN_DV), batch, seq)
        o_b = _moba(proj, rel_bias, batch, seq)
        xf = _mix(o_a, o_b, proj, xf, w_up_a[layer].astype(BF16), w_up_b[layer].astype(BF16),
                  w_o[layer].astype(BF16), ln1_g[layer].reshape(1, d), ln1_b[layer].reshape(1, d))
        i = layer // 2
        g2 = ln2_g[layer].reshape(1, d)
        b2 = ln2_b[layer].reshape(1, d)
        if layer % 2 == 0:
            xf = _ffn(xf, ffn_w_gate[i][None].astype(BF16), ffn_w_up[i][None].astype(BF16),
                      ffn_w_down[i][None].astype(BF16), g2, b2)
        else:
            rw = jnp.pad(router_w[i], ((0, 0), (0, LANES - N_EXPERTS)))
            route = _router(xf, rw, _pad_lanes(router_b[i]))
            xf = _moe(xf, route, ewg, ewu, ewd, i * N_EXPERTS, g2, b2)
    return xf.reshape(batch, seq, d)
```

```python
import functools
import math

import numpy as np
import jax
import jax.numpy as jnp
from jax import lax
from jax.experimental import pallas as pl
from jax.experimental.pallas import tpu as pltpu

F32 = jnp.float32
BF16 = jnp.bfloat16
HIGHEST = lax.Precision.HIGHEST

D_MODEL = 1024
DEPTH = 4
DN_HEADS = 8
DN_DK = 128
DN_DV = 128
DN_CHUNK = 64
CONV_WIDTH = 4
QK_DIM = DN_HEADS * DN_DK
V_DIM = DN_HEADS * DN_DV
N_QKV_DN = 2 * QK_DIM + V_DIM
MOBA_HEADS = 8
MOBA_HD = 64
MOBA_DIM = MOBA_HEADS * MOBA_HD
MOBA_BLOCK = 256
MOBA_TOPK = 3
MOBA_Q_CHUNK = 128
N_BUCKETS = 32
REL_MAX_DIST = 128
N_EXPERTS = 8
DEEPNORM_ALPHA = (2 * DEPTH) ** 0.25
LN_EPS = 1e-5
RMS_EPS = 1e-6

LANES = 128
VMEM_LIMIT = 48 * 1024 * 1024

C_QKV = 0
C_Z = N_QKV_DN
C_GATE = C_Z + V_DIM
C_MB = C_GATE + 2 * D_MODEL
N_PROJ = C_MB + 3 * MOBA_DIM


def _bucket_starts():
    n = np.arange(0, 4 * REL_MAX_DIST)
    max_exact = N_BUCKETS // 2
    t = np.log(np.maximum(n, 1) / max_exact) / math.log(REL_MAX_DIST / max_exact) * (N_BUCKETS - max_exact)
    large = np.minimum(max_exact + t.astype(np.int64), N_BUCKETS - 1)
    b = np.where(n < max_exact, n, large)
    assert np.all(np.diff(b) >= 0) and b[-1] == N_BUCKETS - 1
    return [int(np.argmax(b >= k)) for k in range(N_BUCKETS)]


BUCKET_STARTS = _bucket_starts()
assert BUCKET_STARTS[-1] <= REL_MAX_DIST


def _cparams(sem):
    return pltpu.CompilerParams(dimension_semantics=sem, vmem_limit_bytes=VMEM_LIMIT)


def _mm(a, b, prec=None):
    return jnp.dot(a, b, preferred_element_type=F32, precision=prec)


def _mm_nt(a, b, prec=None):
    return lax.dot_general(a, b, (((1,), (1,)), ((), ())), preferred_element_type=F32, precision=prec)


def _mm_tn(a, b, prec=None):
    return lax.dot_general(a, b, (((0,), (0,)), ((), ())), preferred_element_type=F32, precision=prec)


def _bmm(a, b, prec=None):
    return lax.dot_general(a, b, (((2,), (1,)), ((0,), (0,))), preferred_element_type=F32, precision=prec)


def _bmm_nt(a, b, prec=None):
    return lax.dot_general(a, b, (((2,), (2,)), ((0,), (0,))), preferred_element_type=F32, precision=prec)


def _split(a):
    hi = a.astype(BF16)
    return hi, (a - hi.astype(F32)).astype(BF16)


def _bmm_split(a, b):
    return _bmm(a[0], b[0]) + (_bmm(a[0], b[1]) + _bmm(a[1], b[0]))


def _layer_norm(y, g, b):
    mu = jnp.mean(y, axis=-1, keepdims=True)
    yc = y - mu
    var = jnp.mean(yc * yc, axis=-1, keepdims=True)
    return yc * lax.rsqrt(var + LN_EPS) * g + b


PROJ_TN = 768
PROJ_CHUNK = 256
GATE_ROWS = 256
CARRY_ROWS = 8


def _proj_kernel(x_ref, w_ref, cw_ref, wab_ref, alog_ref, dtb_ref, o_ref, gb_ref, x16_ref, carry_ref,
                 *, tiles_per_seq):
    i = pl.program_id(0)
    j = pl.program_id(1)
    tm = x_ref.shape[0]
    n_conv = N_QKV_DN // PROJ_TN
    heads_per_tile = PROJ_TN // DN_DK

    @pl.when(j == 0)
    def _():
        x16 = x_ref[...].astype(BF16)
        x16_ref[...] = x16
        p = _mm(x16, wab_ref[...])
        z = p + dtb_ref[...]
        g = -jnp.exp(alog_ref[...]) * (jnp.maximum(z, 0.0) + jnp.log1p(jnp.exp(-jnp.abs(z))))
        in_chunk = lax.broadcasted_iota(jnp.int32, g.shape, 0) % DN_CHUNK
        step = 1
        while step < DN_CHUNK:
            g = g + jnp.where(in_chunk >= step, pltpu.roll(g, step, axis=0), 0.0)
            step *= 2
        lane = lax.broadcasted_iota(jnp.int32, g.shape, 1)
        gb_ref[...] = jnp.where(lane < DN_HEADS, g, jax.nn.sigmoid(p))

    @pl.when(j >= n_conv)
    def _():
        o_ref[...] = _mm(x16_ref[...], w_ref[...])

    @pl.when(j < n_conv)
    def _():
        @pl.when(i % tiles_per_seq == 0)
        def _():
            carry_ref[j] = jnp.zeros((CARRY_ROWS, PROJ_TN), F32)

        row = lax.broadcasted_iota(jnp.int32, (CARRY_ROWS, PROJ_CHUNK), 0)
        for c in range(PROJ_TN // PROJ_CHUNK):
            cc = slice(c * PROJ_CHUNK, (c + 1) * PROJ_CHUNK)
            y = _mm(x16_ref[...], w_ref[:, cc])
            w = cw_ref[:, cc]
            prev = carry_ref[j, :, cc]
            carry_ref[j, :, cc] = y[tm - CARRY_ROWS:, :]
            acc = y * w[CONV_WIDTH - 1:CONV_WIDTH, :]
            for k in range(1, CONV_WIDTH):
                rolled = pltpu.roll(y, k, axis=0)
                top = jnp.where(row < k, pltpu.roll(prev, k, axis=0), rolled[:CARRY_ROWS])
                shifted = jnp.concatenate([top, rolled[CARRY_ROWS:]], axis=0)
                acc = acc + shifted * w[CONV_WIDTH - 1 - k:CONV_WIDTH - k, :]
            acc = acc * jax.nn.sigmoid(acc)
            for h in range(PROJ_CHUNK // DN_DK):
                a = acc[:, h * DN_DK:(h + 1) * DN_DK]
                head = j * heads_per_tile + c * (PROJ_CHUNK // DN_DK) + h
                an = a * lax.rsqrt(jnp.sum(a * a, axis=-1, keepdims=True) + RMS_EPS)
                an = an * jnp.where(head < DN_HEADS, DN_DK ** -0.5, 1.0)
                col = c * PROJ_CHUNK + h * DN_DK
                o_ref[:, col:col + DN_DK] = jnp.where(head < 2 * DN_HEADS, an, a)


def _proj(x, w, conv_w, w_ab, alog, dtb, seq, tm=1024):
    t, d = x.shape
    n = w.shape[1]
    tm = min(tm, seq)
    n_conv = N_QKV_DN // PROJ_TN
    const = lambda a: pl.BlockSpec(a.shape, lambda i, j: (0, 0))
    return pl.pallas_call(
        functools.partial(_proj_kernel, tiles_per_seq=seq // tm),
        grid=(t // tm, n // PROJ_TN),
        in_specs=[pl.BlockSpec((tm, d), lambda i, j: (i, 0)),
                  pl.BlockSpec((d, PROJ_TN), lambda i, j: (0, j)),
                  pl.BlockSpec((CONV_WIDTH, PROJ_TN), lambda i, j: (0, jnp.minimum(j, n_conv - 1))),
                  const(w_ab), const(alog), const(dtb)],
        out_specs=[pl.BlockSpec((tm, PROJ_TN), lambda i, j: (i, j)),
                   pl.BlockSpec((tm, LANES), lambda i, j: (i, 0))],
        out_shape=[jax.ShapeDtypeStruct((t, n), F32), jax.ShapeDtypeStruct((t, LANES), F32)],
        scratch_shapes=[pltpu.VMEM((tm, d), BF16), pltpu.VMEM((n_conv, CARRY_ROWS, PROJ_TN), F32)],
        compiler_params=_cparams(("arbitrary", "arbitrary")),
        name="proj",
    )(x, w, conv_w, w_ab, alog, dtb)


def _dn_kernel(q_ref, k_ref, v_ref, z_ref, gb_ref, dct_ref, nw_ref, o_ref, s_ref, *, cb):
    @pl.when(pl.program_id(1) == 0)
    def _():
        s_ref[...] = jnp.zeros_like(s_ref)

    r = lax.broadcasted_iota(jnp.int32, (DN_CHUNK, DN_CHUNK), 0)
    c = lax.broadcasted_iota(jnp.int32, (DN_CHUNK, DN_CHUNK), 1)
    lower = r >= c
    strict = r > c
    eye = (r == c).astype(F32)
    nw = nw_ref[...]

    heads = range(DN_HEADS)
    head_cols = [slice(h * DN_DK, (h + 1) * DN_DK) for h in heads]

    chunk_rows = [slice(ci * DN_CHUNK, (ci + 1) * DN_CHUNK) for ci in range(cb)]
    units = [(ci, h) for ci in range(cb) for h in heads]
    gb = gb_ref[...]
    q = jnp.stack([q_ref[chunk_rows[ci], head_cols[h]] for ci, h in units])
    k = jnp.stack([k_ref[chunk_rows[ci], head_cols[h]] for ci, h in units])
    v = jnp.stack([v_ref[chunk_rows[ci], head_cols[h]] for ci, h in units])
    dc = jnp.stack([gb[chunk_rows[ci], h:h + 1] for ci, h in units])
    beta = jnp.stack([gb[chunk_rows[ci], DN_HEADS + h:DN_HEADS + h + 1] for ci, h in units])
    dcr = dct_ref[...].reshape(cb * DN_HEADS, 1, DN_CHUNK)
    dc_last = dc[:, DN_CHUNK - 1:DN_CHUNK, :]
    e_col = jnp.exp(dc)
    dm = jnp.exp(jnp.where(lower[None], dc - dcr, -jnp.inf))
    kb = k * beta
    vb = v * beta
    k16 = k.astype(BF16)
    kk = _bmm_nt(kb.astype(BF16), k16)
    nmat = jnp.where(strict[None], kk * dm, 0.0)
    pw = _split(-nmat)
    tinv = eye[None] - nmat
    for _ in range(5):
        pw = _split(_bmm_split(pw, pw))
        tinv = tinv + _bmm_split(_split(tinv), pw)
    sol = _bmm_split(_split(tinv), _split(jnp.concatenate([vb, kb * e_col], axis=-1)))
    value = sol[:, :, :DN_DV]
    att16 = (_bmm_nt(q.astype(BF16), k16) * dm).astype(BF16)
    kq16 = jnp.concatenate([sol[:, :, DN_DV:], q * e_col], axis=1).astype(BF16)
    kd16 = (k * jnp.exp(dc_last - dc)).astype(BF16)
    decay = jnp.exp(dc_last)

    s = s_ref[...]
    for ci in range(cb):
        us = slice(ci * DN_HEADS, (ci + 1) * DN_HEADS)
        rs = _bmm(kq16[us], s.astype(BF16))
        vn16 = (value[us] - rs[:, :DN_CHUNK]).astype(BF16)
        o = rs[:, DN_CHUNK:] + _bmm(att16[us], vn16)
        upd = jnp.stack([_mm_tn(kd16[ci * DN_HEADS + h], vn16[h]) for h in heads])
        s = s * decay[us] + upd
        o = o * lax.rsqrt(jnp.mean(o * o, axis=-1, keepdims=True) + RMS_EPS) * nw
        for h in heads:
            zz = z_ref[chunk_rows[ci], head_cols[h]]
            o_ref[chunk_rows[ci], head_cols[h]] = (o[h] * (zz * jax.nn.sigmoid(zz))).astype(o_ref.dtype)
    s_ref[...] = s


def _deltanet(proj, gb, dct, nw, batch, seq, cb=4):
    rows = cb * DN_CHUNK
    nblk = seq // rows
    t = batch * seq
    row_map = lambda col: (lambda b, i: (b * nblk + i, col))
    return pl.pallas_call(
        functools.partial(_dn_kernel, cb=cb),
        grid=(batch, nblk),
        in_specs=[pl.BlockSpec((rows, QK_DIM), row_map(0)),
                  pl.BlockSpec((rows, QK_DIM), row_map(1)),
                  pl.BlockSpec((rows, V_DIM), row_map(2)),
                  pl.BlockSpec((rows, V_DIM), row_map(C_Z // V_DIM)),
                  pl.BlockSpec((rows, LANES), row_map(0)),
                  pl.BlockSpec((cb, DN_HEADS, DN_CHUNK), lambda b, i: (b * nblk + i, 0, 0)),
                  pl.BlockSpec((1, DN_DV), lambda b, i: (0, 0))],
        out_specs=pl.BlockSpec((rows, V_DIM), row_map(0)),
        out_shape=jax.ShapeDtypeStruct((t, V_DIM), BF16),
        scratch_shapes=[pltpu.VMEM((DN_HEADS, DN_DK, DN_DV), F32)],
        compiler_params=_cparams(("parallel", "arbitrary")),
        name="deltanet",
    )(proj, proj, proj, proj, gb, dct, nw)


def _rel_bias_tile(bias_ref, head, dist):
    val = jnp.full(dist.shape, bias_ref[head, 0], F32)
    for b in range(1, N_BUCKETS):
        val = jnp.where(dist >= BUCKET_STARTS[b], bias_ref[head, b], val)
    return val


MOBA_NEG = -1e30
MOBA_PAIR = 2 * MOBA_BLOCK


def _moba_kernel(bias_ref, q_ref, k_ref, v_ref, o_ref, ka_ref, v16_ref, bt_ref, s_ref, qv_ref, *, seq):
    hp = pl.program_id(0)
    nb = seq // MOBA_BLOCK
    k = k_ref[...]
    v16_ref[...] = v_ref[...].astype(BF16)
    kmean = jnp.mean(k.reshape(nb, MOBA_BLOCK, LANES), axis=1)
    lane_k = lax.broadcasted_iota(jnp.int32, k.shape, 1)
    blk_k = lax.broadcasted_iota(jnp.int32, k.shape, 0) // MOBA_BLOCK
    qi = lax.broadcasted_iota(jnp.int32, (MOBA_BLOCK, MOBA_BLOCK), 0)
    kj = lax.broadcasted_iota(jnp.int32, (MOBA_BLOCK, MOBA_BLOCK), 1)
    rel = qi - kj
    for hh in range(2):
        aug = (1 - hh) * MOBA_HD
        in_head_k = (lane_k >= hh * MOBA_HD) & (lane_k < (hh + 1) * MOBA_HD)
        ka_ref[hh] = jnp.where(in_head_k, k, (lane_k - aug == blk_k).astype(F32)).astype(BF16)

    @pl.when(pl.program_id(1) == 0)
    def _():
        for hh in range(2):
            b_far = bias_ref[2 * hp + hh, N_BUCKETS - 1]
            own = _rel_bias_tile(bias_ref, 2 * hp + hh, jnp.maximum(rel, 0)) - b_far
            bt_ref[hh, 0] = jnp.where(rel >= 0, own, -jnp.inf)
            bt_ref[hh, 1] = _rel_bias_tile(bias_ref, 2 * hp + hh, rel + MOBA_BLOCK) - b_far

    lane = lax.broadcasted_iota(jnp.int32, (MOBA_BLOCK, LANES), 1)

    slab = min(4 * MOBA_BLOCK, seq)
    blk_t = lax.broadcasted_iota(jnp.int32, (nb, slab), 0)
    lane_q = lax.broadcasted_iota(jnp.int32, (slab, LANES), 1)
    for sb in range(seq // slab):
        srows = slice(sb * slab, (sb + 1) * slab)
        q2 = q_ref[srows, :] * (MOBA_HD ** -0.5)
        qblk = (lax.broadcasted_iota(jnp.int32, (nb, slab), 1) + sb * slab) // MOBA_BLOCK
        for hh in range(2):
            aug = (1 - hh) * MOBA_HD
            in_head = (lane_q >= hh * MOBA_HD) & (lane_q < (hh + 1) * MOBA_HD)
            qh = jnp.where(in_head, q2, 0.0)
            sc = jnp.where(blk_t < qblk, _mm_nt(kmean, qh, HIGHEST), -jnp.inf)
            sel = jnp.zeros(sc.shape, F32)
            for _ in range(MOBA_TOPK):
                m = jnp.max(sc, axis=0, keepdims=True)
                cand = jnp.where((sc == m) & (sc > -jnp.inf), blk_t, nb)
                pick = blk_t == jnp.min(cand, axis=0, keepdims=True)
                sel = jnp.where(pick, 1.0, sel)
                sc = jnp.where(pick, -jnp.inf, sc)
            pen_near = jnp.where(sel > 0.0, 0.0, MOBA_NEG)
            pen_far = jnp.where(blk_t == qblk - 1, MOBA_NEG, pen_near)

            def place(pen_t):
                parts = [jnp.zeros((aug, slab), F32)] if aug else []
                parts += [pen_t, jnp.zeros((LANES - aug - nb, slab), F32)]
                return jnp.concatenate(parts, axis=0).T

            qv_ref[hh, 0, srows, :] = qh.astype(BF16)
            qv_ref[hh, 1, srows, :] = jnp.where(in_head, q2, place(pen_near)).astype(BF16)
            qv_ref[hh, 2, srows, :] = jnp.where(in_head, q2, place(pen_far)).astype(BF16)

    def lane_tile_max(s):
        parts = [s[:, i * LANES:(i + 1) * LANES] for i in range(s.shape[1] // LANES)]
        return functools.reduce(jnp.maximum, parts)

    def lane_tile_sum(s):
        parts = [s[:, i * LANES:(i + 1) * LANES] for i in range(s.shape[1] // LANES)]
        return functools.reduce(jnp.add, parts)

    def q_block(qb, carry):
        rows = pl.ds(pl.multiple_of(qb * MOBA_BLOCK, MOBA_BLOCK), MOBA_BLOCK)
        own_rows = rows
        near_rows = pl.ds(pl.multiple_of(jnp.maximum(qb - 1, 0) * MOBA_BLOCK, MOBA_BLOCK), MOBA_BLOCK)
        npair = qb // 2
        q_far, mts = [], []
        for hh in range(2):
            q_far.append(qv_ref[hh, 2, rows, :])
            s_own = _mm_nt(qv_ref[hh, 0, rows, :], ka_ref[hh, own_rows, :]) + bt_ref[hh, 0]
            s_near = _mm_nt(qv_ref[hh, 1, rows, :], ka_ref[hh, near_rows, :]) + bt_ref[hh, 1]
            s_ref[hh, 0, :, 0:MOBA_BLOCK] = s_own
            s_ref[hh, 0, :, MOBA_BLOCK:MOBA_PAIR] = s_near
            mts.append(jnp.maximum(lane_tile_max(s_own), lane_tile_max(s_near)))

        def far_scores(p, mt):
            ks = pl.ds(pl.multiple_of(p * MOBA_PAIR, MOBA_PAIR), MOBA_PAIR)
            out = []
            for hh in range(2):
                s = _mm_nt(q_far[hh], ka_ref[hh, ks, :])
                s_ref[hh, 1 + p] = s
                out.append(jnp.maximum(mt[hh], lane_tile_max(s)))
            return tuple(out)

        mts = lax.fori_loop(0, npair, far_scores, tuple(mts))
        ms = [jnp.max(mt, axis=-1, keepdims=True) for mt in mts]

        lts, accs = [], []
        for hh in range(2):
            p = jnp.exp(s_ref[hh, 0] - ms[hh])
            p16 = p.astype(BF16)
            lts.append(lane_tile_sum(p))
            accs.append(_mm(p16[:, 0:MOBA_BLOCK], v16_ref[own_rows, :])
                        + _mm(p16[:, MOBA_BLOCK:MOBA_PAIR], v16_ref[near_rows, :]))

        def far_values(p, st):
            ks = pl.ds(pl.multiple_of(p * MOBA_PAIR, MOBA_PAIR), MOBA_PAIR)
            lt, acc = st
            v_pair = v16_ref[ks, :]
            lt_new, acc_new = [], []
            for hh in range(2):
                pp = jnp.exp(s_ref[hh, 1 + p] - ms[hh])
                lt_new.append(lt[hh] + lane_tile_sum(pp))
                acc_new.append(acc[hh] + _mm(pp.astype(BF16), v_pair))
            return tuple(lt_new), tuple(acc_new)

        lts, accs = lax.fori_loop(0, npair, far_values, (tuple(lts), tuple(accs)))
        outs = [accs[hh] / jnp.sum(lts[hh], axis=-1, keepdims=True) for hh in range(2)]
        o_ref[rows, :] = jnp.where(lane < MOBA_HD, outs[0], outs[1]).astype(o_ref.dtype)
        return carry

    lax.fori_loop(0, nb, q_block, 0)


def _moba(proj, rel_bias, batch, seq):
    npair = MOBA_HEADS // 2
    qb, kb, vb = ((C_MB + i * MOBA_DIM) // LANES for i in range(3))
    blk = lambda base: pl.BlockSpec((seq, LANES), lambda h, b: (b, base + h))
    return pl.pallas_call(
        functools.partial(_moba_kernel, seq=seq),
        grid=(npair, batch),
        in_specs=[pl.BlockSpec(memory_space=pltpu.SMEM), blk(qb), blk(kb), blk(vb)],
        out_specs=pl.BlockSpec((seq, LANES), lambda h, b: (b, h)),
        out_shape=jax.ShapeDtypeStruct((batch * seq, MOBA_DIM), BF16),
        scratch_shapes=[pltpu.VMEM((2, seq, LANES), BF16), pltpu.VMEM((seq, LANES), BF16),
                        pltpu.VMEM((2, 2, MOBA_BLOCK, MOBA_BLOCK), F32),
                        pltpu.VMEM((2, 1 + seq // MOBA_PAIR, MOBA_BLOCK, MOBA_PAIR), F32),
                        pltpu.VMEM((2, 3, seq, LANES), BF16)],
        compiler_params=_cparams(("parallel", "arbitrary")),
        name="moba",
    )(rel_bias, proj, proj, proj)


def _mix_kernel(oa_ref, ob_ref, ga_ref, gb_ref, x_ref, wa_ref, wb_ref, wo_ref, g_ref, b_ref, o_ref):
    ya = _mm(oa_ref[...], wa_ref[...])
    yb = _mm(ob_ref[...], wb_ref[...])
    m = jax.nn.sigmoid(ga_ref[...]) * ya + jax.nn.sigmoid(gb_ref[...]) * yb
    mix = _mm(m.astype(BF16), wo_ref[...])
    o_ref[...] = _layer_norm(DEEPNORM_ALPHA * x_ref[...] + mix, g_ref[...], b_ref[...])


def _mix(oa, ob, proj, x, wa, wb, wo, g, b, tm=512):
    t, d = x.shape
    row = lambda col: (lambda i: (i, col))
    full = lambda a: pl.BlockSpec(a.shape, lambda i: (0, 0))
    return pl.pallas_call(
        _mix_kernel,
        grid=(t // tm,),
        in_specs=[pl.BlockSpec((tm, V_DIM), row(0)), pl.BlockSpec((tm, MOBA_DIM), row(0)),
                  pl.BlockSpec((tm, d), row(C_GATE // d)), pl.BlockSpec((tm, d), row(C_GATE // d + 1)),
                  pl.BlockSpec((tm, d), row(0)), full(wa), full(wb), full(wo), full(g), full(b)],
        out_specs=pl.BlockSpec((tm, d), row(0)),
        out_shape=jax.ShapeDtypeStruct((t, d), F32),
        compiler_params=_cparams(("parallel",)),
        name="mix",
    )(oa, ob, proj, proj, x, wa, wb, wo, g, b)


ROUTE_IDX = 0
ROUTE_W = 2
MOE_TOPK = 2
MOE_TM = 512


def _router_kernel(x_ref, w_ref, b_ref, o_ref):
    xh, xl = _split(x_ref[...])
    wh, wl = _split(w_ref[...])
    logits = _mm(xh, wh) + (_mm(xh, wl) + _mm(xl, wh)) + b_ref[...]
    lane = lax.broadcasted_iota(jnp.int32, logits.shape, 1)
    sc = jnp.where(lane < N_EXPERTS, logits, -jnp.inf)
    m1 = jnp.max(sc, axis=-1, keepdims=True)
    i1 = jnp.min(jnp.where(sc == m1, lane, LANES), axis=-1, keepdims=True)
    sc2 = jnp.where(lane == i1, -jnp.inf, sc)
    m2 = jnp.max(sc2, axis=-1, keepdims=True)
    i2 = jnp.min(jnp.where(sc2 == m2, lane, LANES), axis=-1, keepdims=True)
    e2 = jnp.exp(m2 - m1)
    den = 1.0 + e2
    out = jnp.where(lane == ROUTE_IDX, i1.astype(F32), jnp.where(lane == ROUTE_IDX + 1, i2.astype(F32), 0.0))
    o_ref[...] = jnp.where(lane == ROUTE_W, 1.0 / den, jnp.where(lane == ROUTE_W + 1, e2 / den, out))


def _router(x, w, b, tm=512):
    t, d = x.shape
    return pl.pallas_call(
        _router_kernel,
        grid=(t // tm,),
        in_specs=[pl.BlockSpec((tm, d), lambda i: (i, 0)),
                  pl.BlockSpec((d, LANES), lambda i: (0, 0)),
                  pl.BlockSpec((1, LANES), lambda i: (0, 0))],
        out_specs=pl.BlockSpec((tm, LANES), lambda i: (i, 0)),
        out_shape=jax.ShapeDtypeStruct((t, LANES), F32),
        compiler_params=_cparams(("parallel",)),
        name="router",
    )(x, w, b)


FF_CHUNK = 256


def _swiglu(x16, wg_ref, wu_ref, wd_ref):
    acc = None
    for c in range(wg_ref.shape[-1] // FF_CHUNK):
        cs = slice(c * FF_CHUNK, (c + 1) * FF_CHUNK)
        hg = _mm(x16, wg_ref[0, :, cs])
        hu = _mm(x16, wu_ref[0, :, cs])
        y = _mm((hg * jax.nn.sigmoid(hg) * hu).astype(BF16), wd_ref[0, cs, :])
        acc = y if acc is None else acc + y
    return acc


def _ffn_kernel(x_ref, wg_ref, wu_ref, wd_ref, g_ref, b_ref, o_ref):
    x = x_ref[...]
    y = _swiglu(x.astype(BF16), wg_ref, wu_ref, wd_ref)
    o_ref[...] = _layer_norm(DEEPNORM_ALPHA * x + y, g_ref[...], b_ref[...])


def _ffn(x, wg, wu, wd, g, b, tm=512):
    t, d = x.shape
    whole = lambda a: pl.BlockSpec(a.shape, lambda i: (0,) * a.ndim, pipeline_mode=pl.Buffered(1))
    return pl.pallas_call(
        _ffn_kernel,
        grid=(t // tm,),
        in_specs=[pl.BlockSpec((tm, d), lambda i: (i, 0)), whole(wg), whole(wu), whole(wd), whole(g), whole(b)],
        out_specs=pl.BlockSpec((tm, d), lambda i: (i, 0)),
        out_shape=jax.ShapeDtypeStruct((t, d), F32),
        compiler_params=_cparams(("parallel",)),
        name="ffn",
    )(x, wg, wu, wd, g, b)


DMA_UNROLL = 8


def _row_copy(src_hbm, src_row, dst_ref, dst_row, sem):
    return pltpu.make_async_copy(src_hbm.at[pl.ds(src_row, 1), :], dst_ref.at[pl.ds(dst_row, 1), :], sem)


def _dispatch_kernel(pos_ref, x_ref, init_hbm, xs_hbm, sem, *, rows):
    del init_hbm
    base = pl.program_id(0) * rows

    def issue(r, c):
        for k in range(MOE_TOPK):
            _row_copy(x_ref, r, xs_hbm, pos_ref[MOE_TOPK * (base + r) + k], sem).start(priority=k)
        return c

    def drain(r, c):
        for k in range(MOE_TOPK):
            _row_copy(x_ref, 0, xs_hbm, 0, sem).wait()
        return c

    lax.fori_loop(0, rows, issue, 0, unroll=DMA_UNROLL)
    lax.fori_loop(0, rows, drain, 0, unroll=DMA_UNROLL)


def _dispatch(pos, x, n_rows, rows=512):
    t, d = x.shape
    rows = min(rows, t)
    return pl.pallas_call(
        functools.partial(_dispatch_kernel, rows=rows),
        grid_spec=pltpu.PrefetchScalarGridSpec(
            num_scalar_prefetch=1,
            grid=(t // rows,),
            in_specs=[pl.BlockSpec((rows, d), lambda i, pos: (i, 0)), pl.BlockSpec(memory_space=pl.ANY)],
            out_specs=pl.BlockSpec(memory_space=pl.ANY),
            scratch_shapes=[pltpu.SemaphoreType.DMA(())]),
        out_shape=jax.ShapeDtypeStruct((n_rows, d), F32),
        input_output_aliases={2: 0},
        compiler_params=_cparams(("arbitrary",)),
        name="moe_dispatch",
    )(pos, x, jnp.zeros((n_rows, d), F32))


def _moe_kernel(te_ref, tv_ref, x_ref, wg_ref, wu_ref, wd_ref, o_ref, acc_ref, x16_ref):
    del te_ref
    i = pl.program_id(0)
    j = pl.program_id(1)
    last = j == pl.num_programs(1) - 1
    used = tv_ref[i] != 0

    @pl.when(jnp.logical_and(used, j == 0))
    def _():
        x16_ref[...] = x_ref[...].astype(BF16)
        acc_ref[...] = jnp.zeros_like(acc_ref)

    @pl.when(used)
    def _():
        acc_ref[...] += _swiglu(x16_ref[...], wg_ref, wu_ref, wd_ref)

    @pl.when(jnp.logical_and(used, last))
    def _():
        o_ref[...] = acc_ref[...]

    @pl.when(jnp.logical_and(jnp.logical_not(used), last))
    def _():
        o_ref[...] = jnp.zeros_like(o_ref)


def _moe_experts(tile_expert, tile_used, xs, wg, wu, wd, tf):
    p, d = xs.shape
    f = wg.shape[-1]
    nj = f // tf
    col = lambda i, j, te, tv: jnp.where(tv[i] != 0, j, nj - 1)
    return pl.pallas_call(
        _moe_kernel,
        grid_spec=pltpu.PrefetchScalarGridSpec(
            num_scalar_prefetch=2,
            grid=(p // MOE_TM, nj),
            in_specs=[pl.BlockSpec((MOE_TM, d), lambda i, j, te, tv: (i, 0)),
                      pl.BlockSpec((1, d, tf), lambda i, j, te, tv: (te[i], 0, col(i, j, te, tv))),
                      pl.BlockSpec((1, d, tf), lambda i, j, te, tv: (te[i], 0, col(i, j, te, tv))),
                      pl.BlockSpec((1, tf, d), lambda i, j, te, tv: (te[i], col(i, j, te, tv), 0))],
            out_specs=pl.BlockSpec((MOE_TM, d), lambda i, j, te, tv: (i, 0)),
            scratch_shapes=[pltpu.VMEM((MOE_TM, d), F32), pltpu.VMEM((MOE_TM, d), BF16)]),
        out_shape=jax.ShapeDtypeStruct((p, d), F32),
        compiler_params=_cparams(("parallel", "arbitrary")),
        name="moe_experts",
    )(tile_expert, tile_used, xs, wg, wu, wd)


def _combine_kernel(pos_ref, x_ref, rt_ref, ys_hbm, g_ref, b_ref, o_ref, buf_ref, sem, *, tm):
    base = pl.program_id(0) * tm

    def issue(r, c):
        for k in range(MOE_TOPK):
            _row_copy(ys_hbm, pos_ref[MOE_TOPK * (base + r) + k], buf_ref.at[k], r, sem).start(priority=k)
        return c

    def drain(r, c):
        for k in range(MOE_TOPK):
            _row_copy(ys_hbm, 0, buf_ref.at[k], r, sem).wait()
        return c

    lax.fori_loop(0, tm, issue, 0, unroll=DMA_UNROLL)
    lax.fori_loop(0, tm, drain, 0, unroll=DMA_UNROLL)
    rt = rt_ref[...]
    y = rt[:, ROUTE_W:ROUTE_W + 1] * buf_ref[0] + rt[:, ROUTE_W + 1:ROUTE_W + 2] * buf_ref[1]
    o_ref[...] = _layer_norm(DEEPNORM_ALPHA * x_ref[...] + y, g_ref[...], b_ref[...])


def _combine(pos, x, route, ys, g, b, tm=512):
    t, d = x.shape
    tm = min(tm, t)
    return pl.pallas_call(
        functools.partial(_combine_kernel, tm=tm),
        grid_spec=pltpu.PrefetchScalarGridSpec(
            num_scalar_prefetch=1,
            grid=(t // tm,),
            in_specs=[pl.BlockSpec((tm, d), lambda i, pos: (i, 0)),
                      pl.BlockSpec((tm, LANES), lambda i, pos: (i, 0)),
                      pl.BlockSpec(memory_space=pl.ANY),
                      pl.BlockSpec((1, d), lambda i, pos: (0, 0)),
                      pl.BlockSpec((1, d), lambda i, pos: (0, 0))],
            out_specs=pl.BlockSpec((tm, d), lambda i, pos: (i, 0)),
            scratch_shapes=[pltpu.VMEM((MOE_TOPK, tm, d), F32), pltpu.SemaphoreType.DMA(())]),
        out_shape=jax.ShapeDtypeStruct((t, d), F32),
        compiler_params=_cparams(("arbitrary",)),
        name="moe_combine",
    )(pos, x, route, ys, g, b)


def _moe(x, route, wg, wu, wd, first_expert, g, b, tf=1792):
    t, d = x.shape
    n_assign = MOE_TOPK * t
    n_rows = n_assign + N_EXPERTS * MOE_TM
    n_tiles = n_rows // MOE_TM
    expert = route[:, ROUTE_IDX:ROUTE_IDX + MOE_TOPK].astype(jnp.int32).reshape(n_assign)
    onehot = (expert[:, None] == jnp.arange(N_EXPERTS, dtype=jnp.int32)[None, :]).astype(jnp.int32)
    rank = jnp.sum((jnp.cumsum(onehot, axis=0) - onehot) * onehot, axis=1)
    count = jnp.sum(onehot, axis=0)
    padded = (count + MOE_TM - 1) // MOE_TM * MOE_TM
    group_end = jnp.cumsum(padded)
    group_start = group_end - padded
    pos = (jnp.sum(onehot * group_start[None, :], axis=1) + rank).astype(jnp.int32)
    tile_start = jnp.arange(n_tiles, dtype=jnp.int32) * MOE_TM
    tile_used = (tile_start < group_end[-1]).astype(jnp.int32)
    tile_expert = jnp.sum((tile_start[:, None] >= group_end[None, :]).astype(jnp.int32), axis=1)
    last_expert = jnp.sum((group_end[-1] - 1 >= group_end).astype(jnp.int32))
    tile_expert = (jnp.where(tile_used != 0, tile_expert, last_expert) + first_expert).astype(jnp.int32)
    xs = _dispatch(pos, x, n_rows)
    ys = _moe_experts(tile_expert, tile_used, xs, wg, wu, wd, tf)
    return _combine(pos, x, route, ys, g, b)


def _pad_lanes(v):
    v = v.reshape(1, -1).astype(F32)
    return jnp.pad(v, ((0, 0), (0, LANES - v.shape[1])))


def kernel(x, w_in, conv_w, a_log, dt_bias, dn_norm_w, w_up_a, w_up_b, w_o, rel_bias, ln1_g, ln1_b,
           ln2_g, ln2_b, ffn_w_gate, ffn_w_up, ffn_w_down, router_w, router_b, exp_w_gate, exp_w_up,
           exp_w_down):
    batch, seq, d = x.shape
    t = batch * seq
    assert d == D_MODEL and seq % MOBA_BLOCK == 0 and seq // MOBA_BLOCK <= LANES
    c0 = N_QKV_DN
    c1 = c0 + V_DIM
    c2 = c1 + DN_HEADS
    c3 = c2 + DN_HEADS
    c4 = c3 + 3 * MOBA_DIM
    xf = x.reshape(t, d)
    ewg, ewu, ewd = (a.astype(BF16).reshape((-1,) + a.shape[2:]) for a in (exp_w_gate, exp_w_up, exp_w_down))
    for layer in range(DEPTH):
        w = w_in[layer]
        w_main = jnp.concatenate([w[:, :c1], w[:, c4:], w[:, c3:c4]], axis=1).astype(BF16)
        w_ab = jnp.pad(w[:, c1:c3], ((0, 0), (0, LANES - 2 * DN_HEADS))).astype(BF16)
        proj, gb = _proj(xf, w_main, conv_w[layer], w_ab, _pad_lanes(a_log[layer]), _pad_lanes(dt_bias[layer]), seq)
        dct = gb[:, :DN_HEADS].reshape(t // DN_CHUNK, DN_CHUNK, DN_HEADS).transpose(0, 2, 1)
        o_a = _deltanet(proj, gb, dct, dn_norm_w[layer].reshape(1, DN_DV), batch, seq)
        o_b = _moba(proj, rel_bias, batch, seq)
        xf = _mix(o_a, o_b, proj, xf, w_up_a[layer].astype(BF16), w_up_b[layer].astype(BF16),
                  w_o[layer].astype(BF16), ln1_g[layer].reshape(1, d), ln1_b[layer].reshape(1, d))
        i = layer // 2
        g2 = ln2_g[layer].reshape(1, d)
        b2 = ln2_b[layer].reshape(1, d)
        if layer % 2 == 0:
            xf = _ffn(xf, ffn_w_gate[i][None].astype(BF16), ffn_w_up[i][None].astype(BF16),
                      ffn_w_down[i][None].astype(BF16), g2, b2)
        else:
            rw = jnp.pad(router_w[i], ((0, 0), (0, LANES - N_EXPERTS)))
            route = _router(xf, rw, _pad_lanes(router_b[i]))
            xf = _moe(xf, route, ewg, ewu, ewd, i * N_EXPERTS, g2, b2)
    return xf.reshape(batch, seq, d)
```
